```python
import jax, jax.numpy as jnp
from jax import lax
import numpy as np

D_MODEL = 2048
BATCH = 4
SEQ = 4096
DEPTH = 1
DEC_BATCH = 32
DEC_SEQ = 8
PAST_LEN = 16384
PAGE_SIZE = 128

D_CONV = D_MODEL // 2
CONV_WIDTH = 31
NSA_HEADS = 8
NSA_HEAD_DIM = (D_MODEL // 2) // NSA_HEADS
NSA_KV_HEADS = 2
NSA_GROUP = NSA_HEADS // NSA_KV_HEADS
CMP_BLOCK = 32
CMP_STRIDE = 16
CMP_RATIO = CMP_BLOCK // CMP_STRIDE
CMP_HIDDEN = NSA_HEAD_DIM
SEL_BLOCK = 64
SEL_TOPK = 16
WINDOW = 512
Q_BLOCK = 128
SEL_Q_BLOCK = 32
MEM_TOKENS = 256
MEM_HEADS = 4
MEM_HEAD_DIM = 128
D_FF = 5632
NORM_EPS = 1e-6
NEG_INF = -1e30
FORCE_BONUS = 1e4
D_MIX = D_CONV + NSA_HEADS * NSA_HEAD_DIM
D_KV = 2 * NSA_KV_HEADS * NSA_HEAD_DIM
IN_SIZES = (D_CONV, D_CONV, NSA_HEADS * NSA_HEAD_DIM, D_KV, D_KV, D_KV, 3 * NSA_HEADS)
D_IN = sum(IN_SIZES)

kernel_name = 'hymba_conformer_nsa_decoder_step'


def rmsnorm(x, g):
    xf = x.astype(jnp.float32)
    y = xf * lax.rsqrt(jnp.mean(xf * xf, axis=-1, keepdims=True) + NORM_EPS)
    return (y * g.astype(jnp.float32)).astype(x.dtype)


def layernorm(x, g, b):
    xf = x.astype(jnp.float32)
    xc = xf - jnp.mean(xf, axis=-1, keepdims=True)
    y = xc * lax.rsqrt(jnp.mean(xc * xc, axis=-1, keepdims=True) + NORM_EPS)
    return (y * g.astype(jnp.float32) + b.astype(jnp.float32)).astype(x.dtype)


def swiglu_half(x, g, w1, w3, w2):
    h = rmsnorm(x, g)
    return x + 0.5 * ((jax.nn.silu(h @ w1) * (h @ w3)) @ w2)


def alibi_slopes():
    h = jnp.arange(1, NSA_HEADS + 1, dtype=jnp.float32)
    return jnp.exp2(-8.0 * h / NSA_HEADS).reshape(NSA_KV_HEADS, NSA_GROUP)


def pad_seq(x, m):
    r = (-x.shape[1]) % m
    return jnp.pad(x, [(0, 0), (0, r)] + [(0, 0)] * (x.ndim - 2))


def split_in(z):
    offs = np.cumsum(IN_SIZES)[:-1].tolist()
    return jnp.split(z, offs, axis=-1)


def norm_keys(kv, g):
    return jnp.stack([rmsnorm(kv[:, :, 0], g), kv[:, :, 1]], axis=2)


def attend(q, k, v, mask, dist, slopes):
    s = jnp.einsum('...qgrd,...kgd->...grqk', q, k).astype(jnp.float32) * NSA_HEAD_DIM ** -0.5
    s = s - slopes[:, :, None, None] * dist[..., None, None, :, :]
    m = mask[..., None, None, :, :]
    p = jax.nn.softmax(jnp.where(m, s, NEG_INF), axis=-1) * m
    o = jnp.einsum('...grqk,...kgd->...qgrd', p.astype(v.dtype), v)
    return o, p


def conv_module(a, gate, left, conv_w, conv_b, ln_g, ln_b):
    u = a * jax.nn.sigmoid(gate)
    ext = jnp.concatenate([left.astype(u.dtype), u], axis=1)
    y = lax.conv_general_dilated(ext, conv_w[:, None, :].astype(u.dtype), (1,), 'VALID',
                                 dimension_numbers=('NWC', 'WIO', 'NWC'),
                                 feature_group_count=D_CONV) + conv_b
    y = jax.nn.silu(layernorm(y, ln_g, ln_b))
    return y, ext[:, ext.shape[1] - (CONV_WIDTH - 1):]


def cmp_chunk_terms(k, w1):
    b, l, g, d = k.shape
    ch = k.reshape(b, l // CMP_STRIDE, CMP_STRIDE, g, d)
    return jnp.stack([jnp.einsum('bcjgd,jde->bcge', ch, w1[r * CMP_STRIDE:(r + 1) * CMP_STRIDE])
                      for r in range(CMP_RATIO)])


def cmp_combine(u, pe, w1, w2):
    nc = u.shape[2] - CMP_RATIO + 1
    h = jnp.einsum('jd,jde->e', pe, w1)
    for r in range(CMP_RATIO):
        h = h + u[r, :, r:r + nc]
    return jax.nn.silu(h) @ w2


def cmp_branch(q, q_pos, uk, uv, pe_k, w1_k, w2_k, pe_v, w1_v, w2_v, kn_g, slopes):
    kc = rmsnorm(cmp_combine(uk, pe_k, w1_k, w2_k), kn_g)
    vc = cmp_combine(uv, pe_v, w1_v, w2_v)
    ends = jnp.arange(kc.shape[1]) * CMP_STRIDE + (CMP_BLOCK - 1)
    dist = q_pos[:, None] - ends[None, :]
    return attend(q, kc, vc, dist >= 0, dist.astype(jnp.float32), slopes)


def select_blocks(p, q_pos, n_blocks):
    pg = jnp.sum(p, axis=2)
    nc = pg.shape[-1]
    n_ov = SEL_BLOCK // CMP_STRIDE + CMP_RATIO - 1
    j = jnp.arange(n_blocks)
    i = j[:, None] * (SEL_BLOCK // CMP_STRIDE) - (CMP_RATIO - 1) + jnp.arange(n_ov)[None, :]
    ok = (i >= 0) & (i < nc)
    imp = jnp.sum(jnp.where(ok, jnp.take(pg, jnp.clip(i, 0, nc - 1), axis=-1), 0.0), axis=-1)
    jc = q_pos // SEL_BLOCK
    valid = j[None, :] * SEL_BLOCK <= q_pos[:, None]
    forced = (j[None, :] == 0) | (j[None, :] == jc[:, None]) | (j[None, :] == jc[:, None] - 1)
    score = jnp.where(valid, imp + jnp.where(forced, FORCE_BONUS, 0.0), NEG_INF)
    vals, idx = lax.top_k(score, min(SEL_TOPK, n_blocks))
    return jnp.swapaxes(idx, 1, 2), jnp.swapaxes(vals > 0.5 * NEG_INF, 1, 2)


def sel_branch(q, q_pos, idx, valid, gather, slopes):
    b, t = q.shape[:2]
    c = SEL_Q_BLOCK if t % SEL_Q_BLOCK == 0 else t
    n = t // c

    def chunks(a):
        return jnp.moveaxis(a.reshape((b, n, c) + a.shape[2:]), 1, 0)

    def body(args):
        qc, tc, ic, vc = args
        kv = gather(ic)
        keys_shape = ic.shape[:3] + (ic.shape[3] * SEL_BLOCK, NSA_HEAD_DIM)
        kk = kv[..., 0, :].reshape(keys_shape)
        vv = kv[..., 1, :].reshape(keys_shape)
        s_pos = ic[..., None] * SEL_BLOCK + jnp.arange(SEL_BLOCK)
        dist = tc[None, :, None, None, None] - s_pos
        mask = (vc[..., None] & (dist >= 0)).reshape(keys_shape[:4])
        dist = dist.reshape(keys_shape[:4]).astype(jnp.float32)
        s = jnp.einsum('bcgrd,bcgkd->bcgrk', qc, kk).astype(jnp.float32) * NSA_HEAD_DIM ** -0.5
        s = s - slopes[:, :, None] * dist[:, :, :, None, :]
        p = jax.nn.softmax(jnp.where(mask[:, :, :, None, :], s, NEG_INF), axis=-1)
        return jnp.einsum('bcgrk,bcgkd->bcgrd', p.astype(vv.dtype), vv)

    out = lax.map(body, (chunks(q), q_pos.reshape(n, c), chunks(idx), chunks(valid)))
    return jnp.moveaxis(out, 0, 1).reshape(q.shape)


def make_prompt_gather(kv_sel):
    b = kv_sel.shape[0]
    kvb = pad_seq(kv_sel, SEL_BLOCK).reshape(b, -1, SEL_BLOCK, 2, NSA_KV_HEADS, NSA_HEAD_DIM)
    bi = jnp.arange(b)[:, None, None, None]
    gi = jnp.arange(NSA_KV_HEADS)[None, None, :, None]

    def gather(idx):
        return kvb[bi, idx, :, :, gi]
    return gather


def make_sample_gather(pool, layer, page_table, kv_new):
    b = kv_new.shape[0]
    n_pages = page_table.shape[1]
    bpp = PAGE_SIZE // SEL_BLOCK
    past_blocks = n_pages * bpp
    newb = pad_seq(kv_new, SEL_BLOCK).reshape(b, -1, SEL_BLOCK, 2, NSA_KV_HEADS, NSA_HEAD_DIM)
    n_new = newb.shape[1]
    bi = jnp.arange(b)[:, None, None, None]
    gi = jnp.arange(NSA_KV_HEADS)[None, None, :, None]
    offs = jnp.arange(SEL_BLOCK)

    def gather(idx):
        phys = page_table[bi, jnp.clip(idx // bpp, 0, n_pages - 1)]
        off = (idx % bpp)[..., None] * SEL_BLOCK + offs
        old = pool[layer, phys[..., None], off, :, gi[..., None]]
        new = newb[bi, jnp.clip(idx - past_blocks, 0, n_new - 1), :, :, gi]
        return jnp.where((idx < past_blocks)[..., None, None, None], old.astype(new.dtype), new)
    return gather


def win_branch_prompt(q, kv, slopes):
    b, t = q.shape[:2]
    nqb, wb = t // Q_BLOCK, WINDOW // Q_BLOCK
    kvp = jnp.pad(kv, ((0, 0), (wb * Q_BLOCK, 0), (0, 0), (0, 0), (0, 0)))
    kvp = kvp.reshape(b, nqb + wb, Q_BLOCK, 2, NSA_KV_HEADS, NSA_HEAD_DIM)
    band = jnp.concatenate([kvp[:, o:o + nqb] for o in range(wb + 1)], axis=2)
    blk = jnp.arange(nqb)[:, None]
    t_pos = blk * Q_BLOCK + jnp.arange(Q_BLOCK)
    s_pos = (blk - wb) * Q_BLOCK + jnp.arange((wb + 1) * Q_BLOCK)
    dist = t_pos[:, :, None] - s_pos[:, None, :]
    mask = (s_pos[:, None, :] >= 0) & (dist >= 0) & (dist < WINDOW)
    qb = q.reshape(b, nqb, Q_BLOCK, NSA_KV_HEADS, NSA_GROUP, NSA_HEAD_DIM)
    o, _ = attend(qb, band[:, :, :, 0], band[:, :, :, 1], mask, dist.astype(jnp.float32), slopes)
    return o.reshape(q.shape)


def win_branch_sample(q, q_pos, keys, k_pos, slopes):
    dist = q_pos[:, None] - k_pos[None, :]
    mask = (dist >= 0) & (dist < WINDOW)
    o, _ = attend(q, keys[:, :, 0], keys[:, :, 1], mask, dist.astype(jnp.float32), slopes)
    return o


def nsa_merge(o_c, o_s, o_w, gate_logits):
    g = jax.nn.sigmoid(gate_logits).reshape(gate_logits.shape[:2] + (3, NSA_KV_HEADS, NSA_GROUP, 1))
    o = g[:, :, 0] * o_c + g[:, :, 1] * o_s + g[:, :, 2] * o_w
    return o.reshape(o.shape[:2] + (NSA_HEADS * NSA_HEAD_DIM,))


def mem_kv(mem, src_g, w_kv, k_g):
    kv = (rmsnorm(mem, src_g) @ w_kv).reshape(mem.shape[:2] + (2, MEM_HEADS, MEM_HEAD_DIM))
    return jnp.stack([rmsnorm(kv[:, :, 0], k_g), kv[:, :, 1]], axis=2)


def mem_cross(x, kv, norm_g, w_q, q_g, w_o):
    b, t = x.shape[:2]
    q = rmsnorm((rmsnorm(x, norm_g) @ w_q).reshape(b, t, MEM_HEADS, MEM_HEAD_DIM), q_g)
    s = jnp.einsum('bthd,bmhd->bhtm', q, kv[:, :, 0].astype(q.dtype)).astype(jnp.float32) * MEM_HEAD_DIM ** -0.5
    p = jax.nn.softmax(s, axis=-1).astype(x.dtype)
    o = jnp.einsum('bhtm,bmhd->bthd', p, kv[:, :, 1].astype(x.dtype)).reshape(b, t, MEM_HEADS * MEM_HEAD_DIM)
    return x + o @ w_o


def mix_inputs(x, ffn1_norm, ffn1_w1, ffn1_w3, ffn1_w2, mix_norm, w_in, q_norm, k_norm_sel, k_norm_win):
    x = swiglu_half(x, ffn1_norm, ffn1_w1, ffn1_w3, ffn1_w2)
    a, gate, q, kv_cmp, kv_sel, kv_win, g_logits = split_in(rmsnorm(x, mix_norm) @ w_in)
    b, t = x.shape[:2]
    kv_shape = (b, t, 2, NSA_KV_HEADS, NSA_HEAD_DIM)
    q = rmsnorm(q.reshape(b, t, NSA_KV_HEADS, NSA_GROUP, NSA_HEAD_DIM), q_norm)
    return (x, a, gate, q, kv_cmp.reshape(kv_shape), norm_keys(kv_sel.reshape(kv_shape), k_norm_sel),
            norm_keys(kv_win.reshape(kv_shape), k_norm_win), g_logits)


def finish(x, y_conv, o_nsa, kv_m, w_out, mem_norm, w_mem_q, mem_q_norm, w_mem_o,
           ffn2_norm, ffn2_w1, ffn2_w3, ffn2_w2):
    x = x + jnp.concatenate([y_conv, o_nsa], axis=-1) @ w_out
    x = mem_cross(x, kv_m, mem_norm, w_mem_q, mem_q_norm, w_mem_o)
    return swiglu_half(x, ffn2_norm, ffn2_w1, ffn2_w3, ffn2_w2)


def prompt_layer(x, mem, P):
    (ffn1_norm, ffn1_w1, ffn1_w3, ffn1_w2, mix_norm, w_in, w_out, conv_w, conv_b, conv_ln_g, conv_ln_b,
     q_norm, k_norm_cmp, k_norm_sel, k_norm_win, cmp_k_pe, cmp_k_w1, cmp_k_w2, cmp_v_pe, cmp_v_w1, cmp_v_w2,
     mem_norm, mem_src_norm, w_mem_q, w_mem_kv, mem_q_norm, mem_k_norm, w_mem_o,
     ffn2_norm, ffn2_w1, ffn2_w3, ffn2_w2) = P
    b, t = x.shape[:2]
    slopes = alibi_slopes()
    x, a, gate, q, kv_cmp, kv_sel, kv_win, g_logits = mix_inputs(
        x, ffn1_norm, ffn1_w1, ffn1_w3, ffn1_w2, mix_norm, w_in, q_norm, k_norm_sel, k_norm_win)
    y_conv, conv_state = conv_module(a, gate, jnp.zeros((b, CONV_WIDTH - 1, D_CONV), x.dtype),
                                     conv_w, conv_b, conv_ln_g, conv_ln_b)
    q_pos = jnp.arange(t)
    kc_in = pad_seq(kv_cmp, CMP_STRIDE)
    uk = cmp_chunk_terms(kc_in[:, :, 0], cmp_k_w1)
    uv = cmp_chunk_terms(kc_in[:, :, 1], cmp_v_w1)
    o_c, p_c = cmp_branch(q, q_pos, uk, uv, cmp_k_pe, cmp_k_w1, cmp_k_w2, cmp_v_pe, cmp_v_w1, cmp_v_w2,
                          k_norm_cmp, slopes)
    idx, valid = select_blocks(p_c, q_pos, -(-t // SEL_BLOCK))
    o_s = sel_branch(q, q_pos, idx, valid, make_prompt_gather(kv_sel), slopes)
    o_w = win_branch_prompt(q, kv_win, slopes)
    kv_m = mem_kv(mem, mem_src_norm, w_mem_kv, mem_k_norm)
    y = finish(x, y_conv, nsa_merge(o_c, o_s, o_w, g_logits), kv_m, w_out, mem_norm, w_mem_q, mem_q_norm,
               w_mem_o, ffn2_norm, ffn2_w1, ffn2_w3, ffn2_w2)
    return y, kv_cmp, kv_sel, kv_win[:, t - min(WINDOW, t):], kv_m, conv_state


def sample_layer(x, pool_cmp, pool_sel, layer, page_table, win_buf, kv_m, conv_buf, P):
    (ffn1_norm, ffn1_w1, ffn1_w3, ffn1_w2, mix_norm, w_in, w_out, conv_w, conv_b, conv_ln_g, conv_ln_b,
     q_norm, k_norm_cmp, k_norm_sel, k_norm_win, cmp_k_pe, cmp_k_w1, cmp_k_w2, cmp_v_pe, cmp_v_w1, cmp_v_w2,
     mem_norm, mem_src_norm, w_mem_q, w_mem_kv, mem_q_norm, mem_k_norm, w_mem_o,
     ffn2_norm, ffn2_w1, ffn2_w3, ffn2_w2) = P
    b, t = x.shape[:2]
    past_len = page_table.shape[1] * PAGE_SIZE
    slopes = alibi_slopes()
    x, a, gate, q, kv_cmp, kv_sel, kv_win, g_logits = mix_inputs(
        x, ffn1_norm, ffn1_w1, ffn1_w3, ffn1_w2, mix_norm, w_in, q_norm, k_norm_sel, k_norm_win)
    y_conv, conv_state = conv_module(a, gate, conv_buf, conv_w, conv_b, conv_ln_g, conv_ln_b)
    q_pos = past_len + jnp.arange(t)
    past = pool_cmp[layer, page_table].reshape(b, past_len, 2, NSA_KV_HEADS, NSA_HEAD_DIM).astype(kv_cmp.dtype)
    new = pad_seq(kv_cmp, CMP_STRIDE)
    uk = jnp.concatenate([cmp_chunk_terms(past[:, :, 0], cmp_k_w1), cmp_chunk_terms(new[:, :, 0], cmp_k_w1)], axis=2)
    uv = jnp.concatenate([cmp_chunk_terms(past[:, :, 1], cmp_v_w1), cmp_chunk_terms(new[:, :, 1], cmp_v_w1)], axis=2)
    o_c, p_c = cmp_branch(q, q_pos, uk, uv, cmp_k_pe, cmp_k_w1, cmp_k_w2, cmp_v_pe, cmp_v_w1, cmp_v_w2,
                          k_norm_cmp, slopes)
    idx, valid = select_blocks(p_c, q_pos, past_len // SEL_BLOCK + -(-t // SEL_BLOCK))
    o_s = sel_branch(q, q_pos, idx, valid, make_sample_gather(pool_sel, layer, page_table, kv_sel), slopes)
    wc = win_buf.shape[1]
    keys = jnp.concatenate([win_buf.astype(kv_win.dtype), kv_win], axis=1)
    k_pos = past_len - wc + jnp.arange(wc + t)
    o_w = win_branch_sample(q, q_pos, keys, k_pos, slopes)
    y = finish(x, y_conv, nsa_merge(o_c, o_s, o_w, g_logits), kv_m, w_out, mem_norm, w_mem_q, mem_q_norm,
               w_mem_o, ffn2_norm, ffn2_w1, ffn2_w3, ffn2_w2)
    return y, kv_cmp, kv_sel, keys[:, t:], conv_state


def setup_inputs(seed: int = 0) -> dict:
    key = jax.random.key(seed)
    ks = iter(jax.random.split(key, 64))
    f32 = jnp.float32
    n_pages = PAST_LEN // PAGE_SIZE
    n_pool = (DEC_BATCH * n_pages * 5) // 4
    win_cache = min(WINDOW, PAST_LEN)
    G, HD = NSA_KV_HEADS, NSA_HEAD_DIM

    def nrm(shape, scale=1.0):
        return jax.random.normal(next(ks), shape, f32) * scale

    def gain(n):
        return 1.0 + nrm((DEPTH, n), 0.05)

    page_table = jax.random.permutation(next(ks), n_pool)[:DEC_BATCH * n_pages]
    page_table = page_table.reshape(DEC_BATCH, n_pages).astype(jnp.int32)
    return {
        'x_prompt': nrm((BATCH, SEQ, D_MODEL)),
        'x_sample': nrm((DEC_BATCH, DEC_SEQ, D_MODEL)),
        'cache_kv_cmp': nrm((DEPTH, n_pool, PAGE_SIZE, 2, G, HD)),
        'cache_kv_sel': nrm((DEPTH, n_pool, PAGE_SIZE, 2, G, HD)),
        'cache_kv_win': nrm((DEPTH, DEC_BATCH, win_cache, 2, G, HD)),
        'cache_mem_kv': nrm((DEPTH, DEC_BATCH, MEM_TOKENS, 2, MEM_HEADS, MEM_HEAD_DIM)),
        'state_conv': nrm((DEPTH, DEC_BATCH, CONV_WIDTH - 1, D_CONV)),
        'page_table': page_table,
        'mem_prompt': nrm((BATCH, MEM_TOKENS, D_MODEL)),
        'ffn1_norm': gain(D_MODEL),
        'ffn1_w1': nrm((DEPTH, D_MODEL, D_FF), D_MODEL ** -0.5),
        'ffn1_w3': nrm((DEPTH, D_MODEL, D_FF), D_MODEL ** -0.5),
        'ffn1_w2': nrm((DEPTH, D_FF, D_MODEL), D_FF ** -0.5),
        'mix_norm': gain(D_MODEL),
        'w_in': nrm((DEPTH, D_MODEL, D_IN), D_MODEL ** -0.5),
        'w_out': nrm((DEPTH, D_MIX, D_MODEL), D_MIX ** -0.5),
        'conv_w': nrm((DEPTH, CONV_WIDTH, D_CONV), CONV_WIDTH ** -0.5),
        'conv_b': nrm((DEPTH, D_CONV), 0.02),
        'conv_ln_g': gain(D_CONV),
        'conv_ln_b': nrm((DEPTH, D_CONV), 0.02),
        'q_norm': gain(HD),
        'k_norm_cmp': gain(HD),
        'k_norm_sel': gain(HD),
        'k_norm_win': gain(HD),
        'cmp_k_pe': nrm((DEPTH, CMP_BLOCK, HD), 0.1),
        'cmp_k_w1': nrm((DEPTH, CMP_BLOCK, HD, CMP_HIDDEN), (CMP_BLOCK * HD) ** -0.5),
        'cmp_k_w2': nrm((DEPTH, CMP_HIDDEN, HD), CMP_HIDDEN ** -0.5),
        'cmp_v_pe': nrm((DEPTH, CMP_BLOCK, HD), 0.1),
        'cmp_v_w1': nrm((DEPTH, CMP_BLOCK, HD, CMP_HIDDEN), (CMP_BLOCK * HD) ** -0.5),
        'cmp_v_w2': nrm((DEPTH, CMP_HIDDEN, HD), CMP_HIDDEN ** -0.5),
        'mem_norm': gain(D_MODEL),
        'mem_src_norm': gain(D_MODEL),
        'w_mem_q': nrm((DEPTH, D_MODEL, MEM_HEADS * MEM_HEAD_DIM), D_MODEL ** -0.5),
        'w_mem_kv': nrm((DEPTH, D_MODEL, 2 * MEM_HEADS * MEM_HEAD_DIM), D_MODEL ** -0.5),
        'mem_q_norm': gain(MEM_HEAD_DIM),
        'mem_k_norm': gain(MEM_HEAD_DIM),
        'w_mem_o': nrm((DEPTH, MEM_HEADS * MEM_HEAD_DIM, D_MODEL), (MEM_HEADS * MEM_HEAD_DIM) ** -0.5),
        'ffn2_norm': gain(D_MODEL),
        'ffn2_w1': nrm((DEPTH, D_MODEL, D_FF), D_MODEL ** -0.5),
        'ffn2_w3': nrm((DEPTH, D_MODEL, D_FF), D_MODEL ** -0.5),
        'ffn2_w2': nrm((DEPTH, D_FF, D_MODEL), D_FF ** -0.5),
    }


def reference(x_prompt, x_sample, cache_kv_cmp, cache_kv_sel, cache_kv_win, cache_mem_kv, state_conv,
              page_table, mem_prompt, ffn1_norm, ffn1_w1, ffn1_w3, ffn1_w2, mix_norm, w_in, w_out,
              conv_w, conv_b, conv_ln_g, conv_ln_b, q_norm, k_norm_cmp, k_norm_sel, k_norm_win,
              cmp_k_pe, cmp_k_w1, cmp_k_w2, cmp_v_pe, cmp_v_w1, cmp_v_w2, mem_norm, mem_src_norm,
              w_mem_q, w_mem_kv, mem_q_norm, mem_k_norm, w_mem_o, ffn2_norm, ffn2_w1, ffn2_w3, ffn2_w2):
    yp, ys = x_prompt, x_sample
    p_cmp, p_sel, p_win, p_mem, p_conv = [], [], [], [], []
    s_cmp, s_sel, s_win, s_conv = [], [], [], []
    for layer in range(DEPTH):
        P = (ffn1_norm[layer], ffn1_w1[layer], ffn1_w3[layer], ffn1_w2[layer], mix_norm[layer], w_in[layer],
             w_out[layer], conv_w[layer], conv_b[layer], conv_ln_g[layer], conv_ln_b[layer], q_norm[layer],
             k_norm_cmp[layer], k_norm_sel[layer], k_norm_win[layer], cmp_k_pe[layer], cmp_k_w1[layer],
             cmp_k_w2[layer], cmp_v_pe[layer], cmp_v_w1[layer], cmp_v_w2[layer], mem_norm[layer],
             mem_src_norm[layer], w_mem_q[layer], w_mem_kv[layer], mem_q_norm[layer], mem_k_norm[layer],
             w_mem_o[layer], ffn2_norm[layer], ffn2_w1[layer], ffn2_w3[layer], ffn2_w2[layer])
        yp, kc, ks, kw, km, cs = prompt_layer(yp, mem_prompt, P)
        p_cmp.append(kc); p_sel.append(ks); p_win.append(kw); p_mem.append(km); p_conv.append(cs)
        ys, kc2, ks2, kw2, cs2 = sample_layer(ys, cache_kv_cmp, cache_kv_sel, layer, page_table,
                                              cache_kv_win[layer], cache_mem_kv[layer], state_conv[layer], P)
        s_cmp.append(kc2); s_sel.append(ks2); s_win.append(kw2); s_conv.append(cs2)
    new_kv_cmp_prompt = jnp.stack(p_cmp)
    new_kv_sel_prompt = jnp.stack(p_sel)
    new_kv_win_prompt = jnp.stack(p_win)
    new_mem_kv_prompt = jnp.stack(p_mem)
    new_conv_prompt = jnp.stack(p_conv)
    new_kv_cmp_sample = jnp.stack(s_cmp)
    new_kv_sel_sample = jnp.stack(s_sel)
    new_kv_win_sample = jnp.stack(s_win)
    new_conv_sample = jnp.stack(s_conv)
    return (yp, ys, new_kv_cmp_prompt, new_kv_sel_prompt, new_kv_win_prompt, new_mem_kv_prompt, new_conv_prompt,
            new_kv_cmp_sample, new_kv_sel_sample, new_kv_win_sample, new_conv_sample)
```

```python
import functools

import jax
import jax.numpy as jnp
import numpy as np
from jax import lax
from jax.experimental import pallas as pl
from jax.experimental.pallas import tpu as pltpu

D_MODEL = 2048
DEPTH = 1
PAGE_SIZE = 128
D_CONV = D_MODEL // 2
CONV_WIDTH = 31
NSA_HEADS = 8
NSA_HEAD_DIM = (D_MODEL // 2) // NSA_HEADS
NSA_KV_HEADS = 2
NSA_GROUP = NSA_HEADS // NSA_KV_HEADS
CMP_BLOCK = 32
CMP_STRIDE = 16
CMP_RATIO = CMP_BLOCK // CMP_STRIDE
CMP_HIDDEN = NSA_HEAD_DIM
SEL_BLOCK = 64
SEL_TOPK = 16
WINDOW = 512
Q_BLOCK = 128
SEL_Q_BLOCK = 32
MEM_TOKENS = 256
MEM_HEADS = 4
MEM_HEAD_DIM = 128
D_FF = 5632
NORM_EPS = 1e-6
NEG_INF = -1e30
FORCE_BONUS = 1e4
D_MIX = D_CONV + NSA_HEADS * NSA_HEAD_DIM
D_KV = 2 * NSA_KV_HEADS * NSA_HEAD_DIM
IN_SIZES = (D_CONV, D_CONV, NSA_HEADS * NSA_HEAD_DIM, D_KV, D_KV, D_KV, 3 * NSA_HEADS)
D_IN = sum(IN_SIZES)

VMEM_LIMIT_BYTES = 56 * 1024 * 1024


def _ffn_body(x_ref, g_ref, w1_ref, w3_ref, w2_ref, o_ref, h_ref, *, n_f):
    f = pl.program_id(1)

    @pl.when(f == 0)
    def _():
        x = x_ref[...]
        ms = jnp.mean(x * x, axis=-1, keepdims=True)
        h_ref[...] = (x * lax.rsqrt(ms + NORM_EPS) * g_ref[...]).astype(jnp.bfloat16)
        o_ref[...] = jnp.zeros_like(o_ref)

    h = h_ref[...]
    a = jnp.dot(h, w1_ref[...], preferred_element_type=jnp.float32)
    b = jnp.dot(h, w3_ref[...], preferred_element_type=jnp.float32)
    act = (a * jax.nn.sigmoid(a) * b).astype(jnp.bfloat16)
    o_ref[...] += jnp.dot(act, w2_ref[...], preferred_element_type=jnp.float32)

    @pl.when(f == n_f - 1)
    def _():
        o_ref[...] = x_ref[...] + 0.5 * o_ref[...]


def ffn_half(x2d, g, w1, w3, w2, *, tm, tf=512):
    m, d = x2d.shape
    d_ff = w1.shape[1]
    n_f = d_ff // tf
    return pl.pallas_call(
        functools.partial(_ffn_body, n_f=n_f),
        grid=(m // tm, n_f),
        in_specs=[
            pl.BlockSpec((tm, d), lambda i, f: (i, 0)),
            pl.BlockSpec((1, d), lambda i, f: (0, 0)),
            pl.BlockSpec((d, tf), lambda i, f: (0, f)),
            pl.BlockSpec((d, tf), lambda i, f: (0, f)),
            pl.BlockSpec((tf, d), lambda i, f: (f, 0)),
        ],
        out_specs=pl.BlockSpec((tm, d), lambda i, f: (i, 0)),
        out_shape=jax.ShapeDtypeStruct((m, d), jnp.float32),
        scratch_shapes=[pltpu.VMEM((tm, d), jnp.bfloat16)],
        compiler_params=pltpu.CompilerParams(
            dimension_semantics=("parallel", "arbitrary"),
            vmem_limit_bytes=VMEM_LIMIT_BYTES),
        name="ffn_half",
    )(x2d, g.reshape(1, d), w1, w3, w2)


def swiglu_half(x, g, w1, w3, w2):
    b, t, d = x.shape
    m = b * t
    tm = 512 if m % 512 == 0 else m
    return ffn_half(x.reshape(m, d), g, w1, w3, w2, tm=tm).reshape(b, t, d)


def rmsnorm(x, g):
    xf = x.astype(jnp.float32)
    y = xf * lax.rsqrt(jnp.mean(xf * xf, axis=-1, keepdims=True) + NORM_EPS)
    return (y * g.astype(jnp.float32)).astype(x.dtype)


def layernorm(x, g, b):
    xf = x.astype(jnp.float32)
    xc = xf - jnp.mean(xf, axis=-1, keepdims=True)
    y = xc * lax.rsqrt(jnp.mean(xc * xc, axis=-1, keepdims=True) + NORM_EPS)
    return (y * g.astype(jnp.float32) + b.astype(jnp.float32)).astype(x.dtype)


def alibi_slopes():
    h = jnp.arange(1, NSA_HEADS + 1, dtype=jnp.float32)
    return jnp.exp2(-8.0 * h / NSA_HEADS).reshape(NSA_KV_HEADS, NSA_GROUP)


def pad_seq(x, m):
    r = (-x.shape[1]) % m
    return jnp.pad(x, [(0, 0), (0, r)] + [(0, 0)] * (x.ndim - 2))


def split_in(z):
    offs = np.cumsum(IN_SIZES)[:-1].tolist()
    return jnp.split(z, offs, axis=-1)


def norm_keys(kv, g):
    return jnp.stack([rmsnorm(kv[:, :, 0], g), kv[:, :, 1]], axis=2)


def attend(q, k, v, mask, dist, slopes):
    s = jnp.einsum('...qgrd,...kgd->...grqk', q, k).astype(jnp.float32) * NSA_HEAD_DIM ** -0.5
    s = s - slopes[:, :, None, None] * dist[..., None, None, :, :]
    m = mask[..., None, None, :, :]
    p = jax.nn.softmax(jnp.where(m, s, NEG_INF), axis=-1) * m
    o = jnp.einsum('...grqk,...kgd->...qgrd', p.astype(v.dtype), v)
    return o, p


def conv_module(a, gate, left, conv_w, conv_b, ln_g, ln_b):
    u = a * jax.nn.sigmoid(gate)
    ext = jnp.concatenate([left.astype(u.dtype), u], axis=1)
    y = lax.conv_general_dilated(ext, conv_w[:, None, :].astype(u.dtype), (1,), 'VALID',
                                 dimension_numbers=('NWC', 'WIO', 'NWC'),
                                 feature_group_count=D_CONV) + conv_b
    y = jax.nn.silu(layernorm(y, ln_g, ln_b))
    return y, ext[:, ext.shape[1] - (CONV_WIDTH - 1):]


def cmp_chunk_terms(k, w1):
    b, l, g, d = k.shape
    ch = k.reshape(b, l // CMP_STRIDE, CMP_STRIDE, g, d)
    return jnp.stack([jnp.einsum('bcjgd,jde->bcge', ch, w1[r * CMP_STRIDE:(r + 1) * CMP_STRIDE])
                      for r in range(CMP_RATIO)])


def cmp_combine(u, pe, w1, w2):
    nc = u.shape[2] - CMP_RATIO + 1
    h = jnp.einsum('jd,jde->e', pe, w1)
    for r in range(CMP_RATIO):
        h = h + u[r, :, r:r + nc]
    return jax.nn.silu(h) @ w2


def cmp_branch(q, q_pos, uk, uv, pe_k, w1_k, w2_k, pe_v, w1_v, w2_v, kn_g, slopes):
    kc = rmsnorm(cmp_combine(uk, pe_k, w1_k, w2_k), kn_g)
    vc = cmp_combine(uv, pe_v, w1_v, w2_v)
    ends = jnp.arange(kc.shape[1]) * CMP_STRIDE + (CMP_BLOCK - 1)
    dist = q_pos[:, None] - ends[None, :]
    return attend(q, kc, vc, dist >= 0, dist.astype(jnp.float32), slopes)


def select_blocks(p, q_pos, n_blocks):
    pg = jnp.sum(p, axis=2)
    nc = pg.shape[-1]
    n_ov = SEL_BLOCK // CMP_STRIDE + CMP_RATIO - 1
    j = jnp.arange(n_blocks)
    i = j[:, None] * (SEL_BLOCK // CMP_STRIDE) - (CMP_RATIO - 1) + jnp.arange(n_ov)[None, :]
    ok = (i >= 0) & (i < nc)
    imp = jnp.sum(jnp.where(ok, jnp.take(pg, jnp.clip(i, 0, nc - 1), axis=-1), 0.0), axis=-1)
    jc = q_pos // SEL_BLOCK
    valid = j[None, :] * SEL_BLOCK <= q_pos[:, None]
    forced = (j[None, :] == 0) | (j[None, :] == jc[:, None]) | (j[None, :] == jc[:, None] - 1)
    score = jnp.where(valid, imp + jnp.where(forced, FORCE_BONUS, 0.0), NEG_INF)
    vals, idx = lax.top_k(score, min(SEL_TOPK, n_blocks))
    return jnp.swapaxes(idx, 1, 2), jnp.swapaxes(vals > 0.5 * NEG_INF, 1, 2)


def sel_branch(q, q_pos, idx, valid, gather, slopes):
    b, t = q.shape[:2]
    c = SEL_Q_BLOCK if t % SEL_Q_BLOCK == 0 else t
    n = t // c

    def chunks(a):
        return jnp.moveaxis(a.reshape((b, n, c) + a.shape[2:]), 1, 0)

    def body(args):
        qc, tc, ic, vc = args
        kv = gather(ic)
        keys_shape = ic.shape[:3] + (ic.shape[3] * SEL_BLOCK, NSA_HEAD_DIM)
        kk = kv[..., 0, :].reshape(keys_shape)
        vv = kv[..., 1, :].reshape(keys_shape)
        s_pos = ic[..., None] * SEL_BLOCK + jnp.arange(SEL_BLOCK)
        dist = tc[None, :, None, None, None] - s_pos
        mask = (vc[..., None] & (dist >= 0)).reshape(keys_shape[:4])
        dist = dist.reshape(keys_shape[:4]).astype(jnp.float32)
        s = jnp.einsum('bcgrd,bcgkd->bcgrk', qc, kk).astype(jnp.float32) * NSA_HEAD_DIM ** -0.5
        s = s - slopes[:, :, None] * dist[:, :, :, None, :]
        p = jax.nn.softmax(jnp.where(mask[:, :, :, None, :], s, NEG_INF), axis=-1)
        return jnp.einsum('bcgrk,bcgkd->bcgrd', p.astype(vv.dtype), vv)

    out = lax.map(body, (chunks(q), q_pos.reshape(n, c), chunks(idx), chunks(valid)))
    return jnp.moveaxis(out, 0, 1).reshape(q.shape)


def make_prompt_gather(kv_sel):
    b = kv_sel.shape[0]
    kvb = pad_seq(kv_sel, SEL_BLOCK).reshape(b, -1, SEL_BLOCK, 2, NSA_KV_HEADS, NSA_HEAD_DIM)
    bi = jnp.arange(b)[:, None, None, None]
    gi = jnp.arange(NSA_KV_HEADS)[None, None, :, None]

    def gather(idx):
        return kvb[bi, idx, :, :, gi]
    return gather


def make_sample_gather(pool, layer, page_table, kv_new):
    b = kv_new.shape[0]
    n_pages = page_table.shape[1]
    bpp = PAGE_SIZE // SEL_BLOCK
    past_blocks = n_pages * bpp
    newb = pad_seq(kv_new, SEL_BLOCK).reshape(b, -1, SEL_BLOCK, 2, NSA_KV_HEADS, NSA_HEAD_DIM)
    n_new = newb.shape[1]
    bi = jnp.arange(b)[:, None, None, None]
    gi = jnp.arange(NSA_KV_HEADS)[None, None, :, None]
    offs = jnp.arange(SEL_BLOCK)

    def gather(idx):
        phys = page_table[bi, jnp.clip(idx // bpp, 0, n_pages - 1)]
        off = (idx % bpp)[..., None] * SEL_BLOCK + offs
        old = pool[layer, phys[..., None], off, :, gi[..., None]]
        new = newb[bi, jnp.clip(idx - past_blocks, 0, n_new - 1), :, :, gi]
        return jnp.where((idx < past_blocks)[..., None, None, None], old.astype(new.dtype), new)
    return gather


def win_branch_prompt(q, kv, slopes):
    b, t = q.shape[:2]
    nqb, wb = t // Q_BLOCK, WINDOW // Q_BLOCK
    kvp = jnp.pad(kv, ((0, 0), (wb * Q_BLOCK, 0), (0, 0), (0, 0), (0, 0)))
    kvp = kvp.reshape(b, nqb + wb, Q_BLOCK, 2, NSA_KV_HEADS, NSA_HEAD_DIM)
    band = jnp.concatenate([kvp[:, o:o + nqb] for o in range(wb + 1)], axis=2)
    blk = jnp.arange(nqb)[:, None]
    t_pos = blk * Q_BLOCK + jnp.arange(Q_BLOCK)
    s_pos = (blk - wb) * Q_BLOCK + jnp.arange((wb + 1) * Q_BLOCK)
    dist = t_pos[:, :, None] - s_pos[:, None, :]
    mask = (s_pos[:, None, :] >= 0) & (dist >= 0) & (dist < WINDOW)
    qb = q.reshape(b, nqb, Q_BLOCK, NSA_KV_HEADS, NSA_GROUP, NSA_HEAD_DIM)
    o, _ = attend(qb, band[:, :, :, 0], band[:, :, :, 1], mask, dist.astype(jnp.float32), slopes)
    return o.reshape(q.shape)


def win_branch_sample(q, q_pos, keys, k_pos, slopes):
    dist = q_pos[:, None] - k_pos[None, :]
    mask = (dist >= 0) & (dist < WINDOW)
    o, _ = attend(q, keys[:, :, 0], keys[:, :, 1], mask, dist.astype(jnp.float32), slopes)
    return o


def nsa_merge(o_c, o_s, o_w, gate_logits):
    g = jax.nn.sigmoid(gate_logits).reshape(gate_logits.shape[:2] + (3, NSA_KV_HEADS, NSA_GROUP, 1))
    o = g[:, :, 0] * o_c + g[:, :, 1] * o_s + g[:, :, 2] * o_w
    return o.reshape(o.shape[:2] + (NSA_HEADS * NSA_HEAD_DIM,))


def mem_kv(mem, src_g, w_kv, k_g):
    kv = (rmsnorm(mem, src_g) @ w_kv).reshape(mem.shape[:2] + (2, MEM_HEADS, MEM_HEAD_DIM))
    return jnp.stack([rmsnorm(kv[:, :, 0], k_g), kv[:, :, 1]], axis=2)


def mem_cross(x, kv, norm_g, w_q, q_g, w_o):
    b, t = x.shape[:2]
    q = rmsnorm((rmsnorm(x, norm_g) @ w_q).reshape(b, t, MEM_HEADS, MEM_HEAD_DIM), q_g)
    s = jnp.einsum('bthd,bmhd->bhtm', q, kv[:, :, 0].astype(q.dtype)).astype(jnp.float32) * MEM_HEAD_DIM ** -0.5
    p = jax.nn.softmax(s, axis=-1).astype(x.dtype)
    o = jnp.einsum('bhtm,bmhd->bthd', p, kv[:, :, 1].astype(x.dtype)).reshape(b, t, MEM_HEADS * MEM_HEAD_DIM)
    return x + o @ w_o


def mix_inputs(x, ffn1_norm, ffn1_w1, ffn1_w3, ffn1_w2, mix_norm, w_in, q_norm, k_norm_sel, k_norm_win):
    x = swiglu_half(x, ffn1_norm, ffn1_w1, ffn1_w3, ffn1_w2)
    a, gate, q, kv_cmp, kv_sel, kv_win, g_logits = split_in(rmsnorm(x, mix_norm) @ w_in)
    b, t = x.shape[:2]
    kv_shape = (b, t, 2, NSA_KV_HEADS, NSA_HEAD_DIM)
    q = rmsnorm(q.reshape(b, t, NSA_KV_HEADS, NSA_GROUP, NSA_HEAD_DIM), q_norm)
    return (x, a, gate, q, kv_cmp.reshape(kv_shape), norm_keys(kv_sel.reshape(kv_shape), k_norm_sel),
            norm_keys(kv_win.reshape(kv_shape), k_norm_win), g_logits)


def finish(x, y_conv, o_nsa, kv_m, w_out, mem_norm, w_mem_q, mem_q_norm, w_mem_o,
           ffn2_norm, ffn2_w1, ffn2_w3, ffn2_w2):
    x = x + jnp.concatenate([y_conv, o_nsa], axis=-1) @ w_out
    x = mem_cross(x, kv_m, mem_norm, w_mem_q, mem_q_norm, w_mem_o)
    return swiglu_half(x, ffn2_norm, ffn2_w1, ffn2_w3, ffn2_w2)


def prompt_layer(x, mem, P):
    (ffn1_norm, ffn1_w1, ffn1_w3, ffn1_w2, mix_norm, w_in, w_out, conv_w, conv_b, conv_ln_g, conv_ln_b,
     q_norm, k_norm_cmp, k_norm_sel, k_norm_win, cmp_k_pe, cmp_k_w1, cmp_k_w2, cmp_v_pe, cmp_v_w1, cmp_v_w2,
     mem_norm, mem_src_norm, w_mem_q, w_mem_kv, mem_q_norm, mem_k_norm, w_mem_o,
     ffn2_norm, ffn2_w1, ffn2_w3, ffn2_w2) = P
    b, t = x.shape[:2]
    slopes = alibi_slopes()
    x, a, gate, q, kv_cmp, kv_sel, kv_win, g_logits = mix_inputs(
        x, ffn1_norm, ffn1_w1, ffn1_w3, ffn1_w2, mix_norm, w_in, q_norm, k_norm_sel, k_norm_win)
    y_conv, conv_state = conv_module(a, gate, jnp.zeros((b, CONV_WIDTH - 1, D_CONV), x.dtype),
                                     conv_w, conv_b, conv_ln_g, conv_ln_b)
    q_pos = jnp.arange(t)
    kc_in = pad_seq(kv_cmp, CMP_STRIDE)
    uk = cmp_chunk_terms(kc_in[:, :, 0], cmp_k_w1)
    uv = cmp_chunk_terms(kc_in[:, :, 1], cmp_v_w1)
    o_c, p_c = cmp_branch(q, q_pos, uk, uv, cmp_k_pe, cmp_k_w1, cmp_k_w2, cmp_v_pe, cmp_v_w1, cmp_v_w2,
                          k_norm_cmp, slopes)
    idx, valid = select_blocks(p_c, q_pos, -(-t // SEL_BLOCK))
    o_s = sel_branch(q, q_pos, idx, valid, make_prompt_gather(kv_sel), slopes)
    o_w = win_branch_prompt(q, kv_win, slopes)
    kv_m = mem_kv(mem, mem_src_norm, w_mem_kv, mem_k_norm)
    y = finish(x, y_conv, nsa_merge(o_c, o_s, o_w, g_logits), kv_m, w_out, mem_norm, w_mem_q, mem_q_norm,
               w_mem_o, ffn2_norm, ffn2_w1, ffn2_w3, ffn2_w2)
    return y, kv_cmp, kv_sel, kv_win[:, t - min(WINDOW, t):], kv_m, conv_state


def sample_layer(x, pool_cmp, pool_sel, layer, page_table, win_buf, kv_m, conv_buf, P):
    (ffn1_norm, ffn1_w1, ffn1_w3, ffn1_w2, mix_norm, w_in, w_out, conv_w, conv_b, conv_ln_g, conv_ln_b,
     q_norm, k_norm_cmp, k_norm_sel, k_norm_win, cmp_k_pe, cmp_k_w1, cmp_k_w2, cmp_v_pe, cmp_v_w1, cmp_v_w2,
     mem_norm, mem_src_norm, w_mem_q, w_mem_kv, mem_q_norm, mem_k_norm, w_mem_o,
     ffn2_norm, ffn2_w1, ffn2_w3, ffn2_w2) = P
    b, t = x.shape[:2]
    past_len = page_table.shape[1] * PAGE_SIZE
    slopes = alibi_slopes()
    x, a, gate, q, kv_cmp, kv_sel, kv_win, g_logits = mix_inputs(
        x, ffn1_norm, ffn1_w1, ffn1_w3, ffn1_w2, mix_norm, w_in, q_norm, k_norm_sel, k_norm_win)
    y_conv, conv_state = conv_module(a, gate, conv_buf, conv_w, conv_b, conv_ln_g, conv_ln_b)
    q_pos = past_len + jnp.arange(t)
    past = pool_cmp[layer, page_table].reshape(b, past_len, 2, NSA_KV_HEADS, NSA_HEAD_DIM).astype(kv_cmp.dtype)
    new = pad_seq(kv_cmp, CMP_STRIDE)
    uk = jnp.concatenate([cmp_chunk_terms(past[:, :, 0], cmp_k_w1), cmp_chunk_terms(new[:, :, 0], cmp_k_w1)], axis=2)
    uv = jnp.concatenate([cmp_chunk_terms(past[:, :, 1], cmp_v_w1), cmp_chunk_terms(new[:, :, 1], cmp_v_w1)], axis=2)
    o_c, p_c = cmp_branch(q, q_pos, uk, uv, cmp_k_pe, cmp_k_w1, cmp_k_w2, cmp_v_pe, cmp_v_w1, cmp_v_w2,
                          k_norm_cmp, slopes)
    idx, valid = select_blocks(p_c, q_pos, past_len // SEL_BLOCK + -(-t // SEL_BLOCK))
    o_s = sel_branch(q, q_pos, idx, valid, make_sample_gather(pool_sel, layer, page_table, kv_sel), slopes)
    wc = win_buf.shape[1]
    keys = jnp.concatenate([win_buf.astype(kv_win.dtype), kv_win], axis=1)
    k_pos = past_len - wc + jnp.arange(wc + t)
    o_w = win_branch_sample(q, q_pos, keys, k_pos, slopes)
    y = finish(x, y_conv, nsa_merge(o_c, o_s, o_w, g_logits), kv_m, w_out, mem_norm, w_mem_q, mem_q_norm,
               w_mem_o, ffn2_norm, ffn2_w1, ffn2_w3, ffn2_w2)
    return y, kv_cmp, kv_sel, keys[:, t:], conv_state


def kernel(x_prompt, x_sample, cache_kv_cmp, cache_kv_sel, cache_kv_win, cache_mem_kv, state_conv,
           page_table, mem_prompt, ffn1_norm, ffn1_w1, ffn1_w3, ffn1_w2, mix_norm, w_in, w_out,
           conv_w, conv_b, conv_ln_g, conv_ln_b, q_norm, k_norm_cmp, k_norm_sel, k_norm_win,
           cmp_k_pe, cmp_k_w1, cmp_k_w2, cmp_v_pe, cmp_v_w1, cmp_v_w2, mem_norm, mem_src_norm,
           w_mem_q, w_mem_kv, mem_q_norm, mem_k_norm, w_mem_o, ffn2_norm, ffn2_w1, ffn2_w3, ffn2_w2):
    bf = jnp.bfloat16
    layer = 0
    P = (ffn1_norm[layer], ffn1_w1[layer].astype(bf), ffn1_w3[layer].astype(bf), ffn1_w2[layer].astype(bf),
         mix_norm[layer], w_in[layer],
         w_out[layer], conv_w[layer], conv_b[layer], conv_ln_g[layer], conv_ln_b[layer], q_norm[layer],
         k_norm_cmp[layer], k_norm_sel[layer], k_norm_win[layer], cmp_k_pe[layer], cmp_k_w1[layer],
         cmp_k_w2[layer], cmp_v_pe[layer], cmp_v_w1[layer], cmp_v_w2[layer], mem_norm[layer],
         mem_src_norm[layer], w_mem_q[layer], w_mem_kv[layer], mem_q_norm[layer], mem_k_norm[layer],
         w_mem_o[layer], ffn2_norm[layer], ffn2_w1[layer].astype(bf), ffn2_w3[layer].astype(bf),
         ffn2_w2[layer].astype(bf))
    yp, kc, ks, kw, km, cs = prompt_layer(x_prompt, mem_prompt, P)
    ys, kc2, ks2, kw2, cs2 = sample_layer(x_sample, cache_kv_cmp, cache_kv_sel, layer, page_table,
                                          cache_kv_win[layer], cache_mem_kv[layer], state_conv[layer], P)
    st = lambda a: a[None]
    return (yp, ys, st(kc), st(ks), st(kw), st(km), st(cs), st(kc2), st(ks2), st(kw2), st(cs2))
```

```python
import functools

import jax
import jax.numpy as jnp
import numpy as np
from jax import lax
from jax.experimental import pallas as pl
from jax.experimental.pallas import tpu as pltpu

D_MODEL = 2048
PAGE_SIZE = 128
D_CONV = D_MODEL // 2
CONV_WIDTH = 31
NSA_HEADS = 8
HEAD_DIM = 128
NSA_KV_HEADS = 2
NSA_GROUP = NSA_HEADS // NSA_KV_HEADS
CMP_BLOCK = 32
CMP_STRIDE = 16
SEL_BLOCK = 64
SEL_TOPK = 16
WINDOW = 512
MEM_HEADS = 4
NORM_EPS = 1e-6
NEG_INF = -1e30
PICKED = -3e38
FORCE_BONUS = 1e4
D_Q = NSA_HEADS * HEAD_DIM
D_KV = 2 * NSA_KV_HEADS * HEAD_DIM
CHUNK_LANES = CMP_STRIDE * D_KV
ATT_SCALE = HEAD_DIM ** -0.5
LANES = 128
CONV_HALO = 32

VMEM_LIMIT_BYTES = 56 * 1024 * 1024
BF = jnp.bfloat16
F32 = jnp.float32


def _params(*sem):
    return pltpu.CompilerParams(dimension_semantics=sem, vmem_limit_bytes=VMEM_LIMIT_BYTES)


def _resident(shape):
    nd = len(shape)
    return pl.BlockSpec(shape, lambda *_: (0,) * nd, pipeline_mode=pl.Buffered(1))


def _rms(x, g):
    return x * lax.rsqrt(jnp.mean(x * x, axis=-1, keepdims=True) + NORM_EPS) * g


def _dot(a, b):
    return jnp.dot(a, b, preferred_element_type=F32)


def _dot_nt(a, b):
    return lax.dot_general(a, b, (((1,), (1,)), ((), ())), preferred_element_type=F32)


def _iota(shape, dim):
    return lax.broadcasted_iota(jnp.int32, shape, dim)


def _div(x, n):
    return x >> (n.bit_length() - 1)


def _mod(x, n):
    return x & (n - 1)


def _split3(x):
    hi = x.astype(BF)
    r1 = x - hi.astype(F32)
    mid = r1.astype(BF)
    lo = (r1 - mid.astype(F32)).astype(BF)
    return hi, mid, lo


def _ffn_body(x_ref, g_ref, w1_ref, w3_ref, w2_ref, o_ref, h_ref, *, n_f):
    f = pl.program_id(1)

    @pl.when(f == 0)
    def _():
        h_ref[...] = _rms(x_ref[...], g_ref[...]).astype(BF)
        o_ref[...] = jnp.zeros_like(o_ref)

    h = h_ref[...]
    a = _dot(h, w1_ref[...])
    b = _dot(h, w3_ref[...])
    act = (a * jax.nn.sigmoid(a) * b).astype(BF)
    o_ref[...] += _dot(act, w2_ref[...])

    @pl.when(f == n_f - 1)
    def _():
        o_ref[...] = x_ref[...] + 0.5 * o_ref[...]


def ffn_half(x2d, g, w1, w3, w2, *, tm, tf=512):
    m, d = x2d.shape
    n_f = w1.shape[1] // tf
    return pl.pallas_call(
        functools.partial(_ffn_body, n_f=n_f),
        grid=(m // tm, n_f),
        in_specs=[
            pl.BlockSpec((tm, d), lambda i, f: (i, 0)),
            pl.BlockSpec((1, d), lambda i, f: (0, 0)),
            pl.BlockSpec((d, tf), lambda i, f: (0, f)),
            pl.BlockSpec((d, tf), lambda i, f: (0, f)),
            pl.BlockSpec((tf, d), lambda i, f: (f, 0)),
        ],
        out_specs=pl.BlockSpec((tm, d), lambda i, f: (i, 0)),
        out_shape=jax.ShapeDtypeStruct((m, d), F32),
        scratch_shapes=[pltpu.VMEM((tm, d), BF)],
        compiler_params=_params("parallel", "arbitrary"),
        name="ffn_half",
    )(x2d, g.reshape(1, d), w1, w3, w2)


def _in_proj_body(x_ref, g_ref, w_ref, wg_ref, hg_ref, u_ref, q_ref, kc_ref, ks_ref, kw_ref, gl_ref):
    h = _rms(x_ref[...], g_ref[...]).astype(BF)
    a = _dot(h, w_ref[:, 0:D_CONV])
    gate = _dot(h, w_ref[:, D_CONV:2 * D_CONV])
    u_ref[...] = a * jax.nn.sigmoid(gate)
    o = 2 * D_CONV
    for hd in range(NSA_HEADS):
        z = _dot(h, w_ref[:, o + hd * HEAD_DIM:o + (hd + 1) * HEAD_DIM])
        q_ref[:, hd * HEAD_DIM:(hd + 1) * HEAD_DIM] = _rms(z, hg_ref[0:1, :]).astype(BF)
    o += D_Q
    kc_ref[...] = _dot(h, w_ref[:, o:o + D_KV])
    for n, ref in ((1, ks_ref), (2, kw_ref)):
        o += D_KV
        for c in range(2 * NSA_KV_HEADS):
            z = _dot(h, w_ref[:, o + c * HEAD_DIM:o + (c + 1) * HEAD_DIM])
            if c < NSA_KV_HEADS:
                z = _rms(z, hg_ref[n:n + 1, :])
            ref[:, c * HEAD_DIM:(c + 1) * HEAD_DIM] = z
    gl_ref[...] = _dot(h, wg_ref[...])


def in_proj(x2d, g, w_main, w_gate, head_gains, *, tm):
    m, d = x2d.shape
    row = lambda n: pl.BlockSpec((tm, n), lambda i: (i, 0))
    return pl.pallas_call(
        _in_proj_body,
        grid=(m // tm,),
        in_specs=[row(d), _resident((1, d)), _resident(w_main.shape), _resident(w_gate.shape),
                  _resident(head_gains.shape)],
        out_specs=[row(D_CONV), row(D_Q), row(D_KV), row(D_KV), row(D_KV), row(2 * LANES)],
        out_shape=[jax.ShapeDtypeStruct((m, D_CONV), F32), jax.ShapeDtypeStruct((m, D_Q), BF),
                   jax.ShapeDtypeStruct((m, D_KV), F32), jax.ShapeDtypeStruct((m, D_KV), F32),
                   jax.ShapeDtypeStruct((m, D_KV), F32), jax.ShapeDtypeStruct((m, 2 * LANES), F32)],
        compiler_params=_params("parallel"),
        name="in_proj",
    )(x2d, g.reshape(1, d), w_main, w_gate, head_gains)


CONV_ROWS = 32


def _conv_body(u_ref, halo_ref, left_ref, w_ref, b_ref, lg_ref, lb_ref, y_ref, win_ref, *, tt, tiled):
    i = pl.program_id(1)

    @pl.when(i == 0)
    def _():
        win_ref[0:CONV_HALO, :] = left_ref[...]

    if tiled:
        @pl.when(i > 0)
        def _():
            win_ref[0:CONV_HALO, :] = halo_ref[...]

    win_ref[CONV_HALO:CONV_HALO + tt, :] = u_ref[...]
    first = CONV_HALO - (CONV_WIDTH - 1)
    for r0 in range(0, tt, CONV_ROWS):
        rows = min(CONV_ROWS, tt - r0)
        acc = jnp.broadcast_to(b_ref[...], (rows, D_CONV))
        for k in range(CONV_WIDTH):
            acc = acc + w_ref[k:k + 1, :] * win_ref[r0 + first + k:r0 + first + k + rows, :]
        xc = acc - jnp.mean(acc, axis=-1, keepdims=True)
        yn = xc * lax.rsqrt(jnp.mean(xc * xc, axis=-1, keepdims=True) + NORM_EPS) * lg_ref[...] + lb_ref[...]
        y_ref[r0:r0 + rows, :] = (yn * jax.nn.sigmoid(yn)).astype(BF)


def conv_module(u, left, conv_w, conv_b, ln_g, ln_b, *, tt):
    b, t, c = u.shape
    hb = tt // CONV_HALO if tt >= CONV_HALO else 1
    halo_rows = min(CONV_HALO, t)
    w_pad = jnp.pad(conv_w, ((0, CONV_HALO - CONV_WIDTH), (0, 0)))
    return pl.pallas_call(
        functools.partial(_conv_body, tt=tt, tiled=t > tt),
        grid=(b, t // tt),
        in_specs=[
            pl.BlockSpec((None, tt, c), lambda bi, i: (bi, i, 0)),
            pl.BlockSpec((None, halo_rows, c), lambda bi, i: (bi, jnp.maximum(i * hb - 1, 0), 0)),
            pl.BlockSpec((None, CONV_HALO, c), lambda bi, i: (bi, 0, 0)),
            _resident(w_pad.shape), _resident((1, c)), _resident((1, c)), _resident((1, c)),
        ],
        out_specs=pl.BlockSpec((None, tt, c), lambda bi, i: (bi, i, 0)),
        out_shape=jax.ShapeDtypeStruct((b, t, c), BF),
        scratch_shapes=[pltpu.VMEM((CONV_HALO + tt, c), F32)],
        compiler_params=_params("parallel", "arbitrary"),
        name="conv_module",
    )(u, u, left, w_pad, conv_b.reshape(1, c), ln_g.reshape(1, c), ln_b.reshape(1, c))


def _cmp_terms_body(*refs, n_src, n_prefetch=0):
    src = refs[n_prefetch:n_prefetch + n_src]
    wk_ref, wv_ref, o_ref = refs[n_prefetch + n_src:]
    half = CMP_STRIDE * HEAD_DIM
    for c in range(2 * NSA_KV_HEADS):
        parts = [jnp.concatenate([s[:, j * D_KV + c * HEAD_DIM:j * D_KV + (c + 1) * HEAD_DIM]
                                  for j in range(CMP_STRIDE)], axis=1) for s in src]
        x = (parts[0] if n_src == 1 else jnp.concatenate(parts, axis=0)).astype(BF)
        w_ref = wk_ref if c < NSA_KV_HEADS else wv_ref
        for r in range(2):
            o_ref[:, (2 * c + r) * HEAD_DIM:(2 * c + r + 1) * HEAD_DIM] = _dot(x, w_ref[r * half:(r + 1) * half, :])


def cmp_terms_dense(x, w1k, w1v, *, rows):
    b, n, l = x.shape
    return pl.pallas_call(
        functools.partial(_cmp_terms_body, n_src=1),
        grid=(b, n // rows),
        in_specs=[pl.BlockSpec((None, rows, l), lambda bi, i: (bi, i, 0)),
                  _resident(w1k.shape), _resident(w1v.shape)],
        out_specs=pl.BlockSpec((None, rows, 2 * D_KV), lambda bi, i: (bi, i, 0)),
        out_shape=jax.ShapeDtypeStruct((b, n, 2 * D_KV), F32),
        compiler_params=_params("parallel", "parallel"),
        name="cmp_terms_dense",
    )(x, w1k, w1v)


CMP_PAGES_PER_STEP = 16


def cmp_terms_paged(pool, page_table, w1k, w1v):
    _, cpp, l = pool.shape
    b, n_pages = page_table.shape
    pps = CMP_PAGES_PER_STEP
    rows = pps * cpp

    def page_spec(k):
        return pl.BlockSpec((None, cpp, l), lambda bi, i, pt: (pt[bi, i * pps + k], 0, 0))

    return pl.pallas_call(
        functools.partial(_cmp_terms_body, n_src=pps, n_prefetch=1),
        grid_spec=pltpu.PrefetchScalarGridSpec(
            num_scalar_prefetch=1,
            grid=(b, n_pages // pps),
            in_specs=[page_spec(k) for k in range(pps)]
            + [pl.BlockSpec(w1k.shape, lambda bi, i, pt: (0, 0), pipeline_mode=pl.Buffered(1)),
               pl.BlockSpec(w1v.shape, lambda bi, i, pt: (0, 0), pipeline_mode=pl.Buffered(1))],
            out_specs=pl.BlockSpec((None, rows, 2 * D_KV), lambda bi, i, pt: (bi, i, 0)),
        ),
        out_shape=jax.ShapeDtypeStruct((b, n_pages * cpp, 2 * D_KV), F32),
        compiler_params=_params("parallel", "parallel"),
        name="cmp_terms_paged",
    )(page_table, *([pool] * pps), w1k, w1v)


def _cmp_combine_body(u_ref, x_ref, pek_ref, pev_ref, wk1_ref, wv1_ref, wk2_ref, wv2_ref, kg_ref, kc_ref, vc_ref, *, n):
    last = _iota((n, 1), 0) == n - 1
    for c in range(2 * NSA_KV_HEADS):
        is_k = c < NSA_KV_HEADS
        pe_ref, w1_ref, w2_ref = (pek_ref, wk1_ref, wk2_ref) if is_k else (pev_ref, wv1_ref, wv2_ref)
        h0 = _dot(pe_ref[...].astype(BF), w1_ref[...])[0:1, :]
        u0 = u_ref[:, (2 * c) * HEAD_DIM:(2 * c + 1) * HEAD_DIM]
        u1 = u_ref[:, (2 * c + 1) * HEAD_DIM:(2 * c + 2) * HEAD_DIM]
        nxt = x_ref[0:1, (2 * c + 1) * HEAD_DIM:(2 * c + 2) * HEAD_DIM]
        u1 = jnp.where(last, nxt, pltpu.roll(u1, n - 1, axis=0))
        hid = h0 + u0 + u1
        z = _dot((hid * jax.nn.sigmoid(hid)).astype(BF), w2_ref[...])
        g = c % NSA_KV_HEADS
        if is_k:
            kc_ref[g] = _rms(z, kg_ref[...]).astype(BF)
        else:
            vc_ref[g] = z.astype(BF)


def cmp_combine(u, u_next, pe_k, pe_v, w1k, w1v, w2k, w2v, k_gain):
    b, n, l = u.shape
    out = jax.ShapeDtypeStruct((b, NSA_KV_HEADS, n, HEAD_DIM), BF)
    ospec = pl.BlockSpec((None, NSA_KV_HEADS, n, HEAD_DIM), lambda bi: (bi, 0, 0, 0))
    return pl.pallas_call(
        functools.partial(_cmp_combine_body, n=n),
        grid=(b,),
        in_specs=[pl.BlockSpec((None, n, l), lambda bi: (bi, 0, 0)),
                  pl.BlockSpec((None, 8, l), lambda bi: (bi, 0, 0)),
                  _resident(pe_k.shape), _resident(pe_v.shape), _resident(w1k.shape), _resident(w1v.shape),
                  _resident(w2k.shape), _resident(w2v.shape), _resident((1, HEAD_DIM))],
        out_specs=[ospec, ospec],
        out_shape=[out, out],
        compiler_params=_params("parallel"),
        name="cmp_combine",
    )(u, u_next, pe_k, pe_v, w1k, w1v, w2k, w2v, k_gain.reshape(1, HEAD_DIM))


def _softmax_rows(s, mask):
    s = jnp.where(mask, s, NEG_INF)
    m = jnp.max(s, axis=-1, keepdims=True)
    e = jnp.where(mask, jnp.exp(s - m), 0.0)
    l = jnp.sum(e, axis=-1, keepdims=True)
    return e * jnp.where(l > 0.0, 1.0 / l, 0.0)


def _block_scores(pg, ov, q_pos, n_blocks):
    hi, mid, lo = _split3(pg)
    imp = _dot(hi, ov) + _dot(mid, ov) + _dot(lo, ov)
    j = _iota(imp.shape, 1)
    jc = _div(q_pos, SEL_BLOCK)
    valid = j * SEL_BLOCK <= q_pos
    forced = (j == 0) | (j == jc) | (j == jc - 1)
    score = jnp.where(valid, imp + jnp.where(forced, FORCE_BONUS, 0.0), NEG_INF)
    return jnp.where(j < n_blocks, score, PICKED), valid


def _top_blocks(score):
    j = _iota(score.shape, 1).astype(F32)
    sel = jnp.zeros(score.shape, jnp.bool_)
    picks = []
    for _ in range(SEL_TOPK):
        m = jnp.max(score, axis=-1, keepdims=True)
        idx = jnp.min(jnp.where(score == m, j, float(score.shape[1])), axis=-1, keepdims=True)
        one = j == idx
        sel = sel | one
        score = jnp.where(one, PICKED, score)
        picks.append(idx)
    return sel, picks


def _flash(q_rows, slopes, k_ref, v_ref, t0, tq, tk, kt_lo, kt_hi, mask_fn, m_ref, l_ref, acc_ref):
    nh = len(q_rows)
    m_ref[...] = jnp.full(m_ref.shape, NEG_INF, F32)
    l_ref[...] = jnp.zeros(l_ref.shape, F32)
    acc_ref[...] = jnp.zeros(acc_ref.shape, F32)

    def body(kt, carry):
        k0 = pl.multiple_of(kt * tk, tk)
        k = k_ref[pl.ds(k0, tk), :].astype(BF)
        v = v_ref[pl.ds(k0, tk), :].astype(BF)
        dist = (t0 + _iota((tq, tk), 0)) - (k0 + _iota((tq, tk), 1))
        mask = mask_fn(kt, dist)
        dist_f = dist.astype(F32)
        for r in range(nh):
            s = _dot_nt(q_rows[r], k) * ATT_SCALE - slopes[r] * dist_f
            s = jnp.where(mask, s, NEG_INF)
            m_old = m_ref[r]
            m_new = jnp.maximum(m_old, jnp.max(s, axis=-1, keepdims=True))
            alpha = jnp.exp(m_old - m_new)
            p = jnp.where(mask, jnp.exp(s - m_new), 0.0)
            l_ref[r] = alpha * l_ref[r] + jnp.sum(p, axis=-1, keepdims=True)
            acc_ref[r] = alpha * acc_ref[r] + _dot(p.astype(BF), v)
            m_ref[r] = m_new
        return carry

    lax.fori_loop(kt_lo, kt_hi, body, 0)
    outs = []
    for r in range(nh):
        l = l_ref[r]
        outs.append(acc_ref[r] * jnp.where(l > 0.0, 1.0 / l, 0.0))
    return outs


SEL_TK = 512
WIN_TK = 256


def _nsa_prompt_body(slope_ref, q_ref, kc_ref, vc_ref, ks_ref, vs_ref, kw_ref, vw_ref, gl_ref, ov_ref, o_ref,
                     oc_ref, m_ref, l_ref, acc_ref, *, tq, n_cmp, n_blocks):
    g = pl.program_id(1)
    qt = pl.program_id(2)
    t0 = qt * tq
    q_rows = [q_ref[:, r * HEAD_DIM:(r + 1) * HEAD_DIM] for r in range(NSA_GROUP)]
    slopes = [slope_ref[g * NSA_GROUP + r] for r in range(NSA_GROUP)]
    q_pos = t0 + _iota((tq, 1), 0)

    kc = kc_ref[...]
    vc = vc_ref[...]
    n_pad = kc.shape[0]
    i = _iota((tq, n_pad), 1)
    dist = q_pos - (i * CMP_STRIDE + (CMP_BLOCK - 1))
    mask = (dist >= 0) & (i < n_cmp)
    dist_f = dist.astype(F32)
    pg = jnp.zeros((tq, n_pad), F32)
    for r in range(NSA_GROUP):
        p = _softmax_rows(_dot_nt(q_rows[r], kc) * ATT_SCALE - slopes[r] * dist_f, mask)
        pg = pg + p
        oc_ref[r] = _dot(p.astype(BF), vc)
    score, valid = _block_scores(pg, ov_ref[...], q_pos, n_blocks)
    sel, _ = _top_blocks(score)
    sel_bf = jnp.where(sel & valid, 1.0, 0.0).astype(BF)

    bpt = SEL_TK // SEL_BLOCK
    nb_l = sel_bf.shape[1]

    def sel_mask(kt, dist):
        expand = jnp.where(_iota((nb_l, SEL_TK), 0) == kt * bpt + _div(_iota((nb_l, SEL_TK), 1), SEL_BLOCK), 1.0, 0.0)
        return (_dot(sel_bf, expand.astype(BF)) > 0.5) & (dist >= 0)

    o_sel = _flash(q_rows, slopes, ks_ref, vs_ref, t0, tq, SEL_TK, 0, (t0 + tq + SEL_TK - 1) // SEL_TK,
                   sel_mask, m_ref, l_ref, acc_ref)

    def win_mask(kt, dist):
        return (dist >= 0) & (dist < WINDOW)

    o_win = _flash(q_rows, slopes, kw_ref, vw_ref, t0, tq, WIN_TK, jnp.maximum(t0 - (WINDOW - 1), 0) // WIN_TK,
                   (t0 + tq + WIN_TK - 1) // WIN_TK, win_mask, m_ref, l_ref, acc_ref)

    gates = jax.nn.sigmoid(gl_ref[...])
    for r in range(NSA_GROUP):
        gate = lambda c: gates[:, c * NSA_GROUP + r:c * NSA_GROUP + r + 1]
        o_ref[:, r * HEAD_DIM:(r + 1) * HEAD_DIM] = (
            gate(0) * oc_ref[r] + gate(1) * o_sel[r] + gate(2) * o_win[r]).astype(BF)


def nsa_prompt(q, kc, vc, kv_sel, kv_win, glog, slopes, overlap, *, tq):
    b, t, _ = q.shape
    n_pad = kc.shape[2]
    n_cmp = t // CMP_STRIDE - 1
    n_blocks = t // SEL_BLOCK
    gq = NSA_GROUP * HEAD_DIM
    kv = lambda slot: pl.BlockSpec((None, t, HEAD_DIM), lambda bi, g, i: (bi, 0, slot * NSA_KV_HEADS + g))
    cmp_spec = pl.BlockSpec((None, None, n_pad, HEAD_DIM), lambda bi, g, i: (bi, g, 0, 0))
    return pl.pallas_call(
        functools.partial(_nsa_prompt_body, tq=tq, n_cmp=n_cmp, n_blocks=n_blocks),
        grid=(b, NSA_KV_HEADS, t // tq),
        in_specs=[pl.BlockSpec(memory_space=pltpu.SMEM),
                  pl.BlockSpec((None, tq, gq), lambda bi, g, i: (bi, i, g)),
                  cmp_spec, cmp_spec, kv(0), kv(1), kv(0), kv(1),
                  pl.BlockSpec((None, tq, LANES), lambda bi, g, i: (bi, i, g)),
                  pl.BlockSpec(overlap.shape, lambda bi, g, i: (0, 0), pipeline_mode=pl.Buffered(1))],
        out_specs=pl.BlockSpec((None, tq, gq), lambda bi, g, i: (bi, i, g)),
        out_shape=jax.ShapeDtypeStruct((b, t, D_Q), BF),
        scratch_shapes=[pltpu.VMEM((NSA_GROUP, tq, HEAD_DIM), F32), pltpu.VMEM((NSA_GROUP, tq, 1), F32),
                        pltpu.VMEM((NSA_GROUP, tq, 1), F32), pltpu.VMEM((NSA_GROUP, tq, HEAD_DIM), F32)],
        compiler_params=_params("parallel", "parallel", "arbitrary"),
        name="nsa_prompt",
    )(slopes, q, kc, vc, kv_sel, kv_sel, kv_win, kv_win, glog, overlap)


def _overlap_matrix(n_cmp_pad, n_cmp, n_blk_pad, n_blocks):
    i = np.arange(n_cmp_pad)[:, None]
    j = np.arange(n_blk_pad)[None, :]
    per = SEL_BLOCK // CMP_STRIDE
    lo = j * per - (CMP_BLOCK // CMP_STRIDE - 1)
    ov = (i >= lo) & (i < lo + per + CMP_BLOCK // CMP_STRIDE - 1) & (i < n_cmp) & (j < n_blocks)
    return jnp.asarray(ov, BF)


def _sample_rows(t_new, past_len, slope_ref, g):
    rows = NSA_GROUP * t_new
    ri = _iota((rows, 1), 0)
    q_pos = past_len + _mod(ri, t_new)
    slope = jnp.zeros((rows, 1), F32)
    for r in range(NSA_GROUP):
        slope = jnp.where(_div(ri, t_new) == r, slope_ref[g * NSA_GROUP + r], slope)
    return q_pos, slope


def _nsa_cmp_sample_body(slope_ref, q_ref, kc_ref, vc_ref, ov_ref, oc_ref, idx_ref, *, t_new, past_len, n_blocks):
    g = pl.program_id(1)
    q = q_ref[...]
    kc = kc_ref[...]
    n = kc.shape[0]
    q_pos, slope = _sample_rows(t_new, past_len, slope_ref, g)
    dist = q_pos - (_iota((q.shape[0], n), 1) * CMP_STRIDE + (CMP_BLOCK - 1))
    p = _softmax_rows(_dot_nt(q, kc) * ATT_SCALE - slope * dist.astype(F32), dist >= 0)
    oc_ref[...] = _dot(p.astype(BF), vc_ref[...])
    pg = p[0:t_new]
    for r in range(1, NSA_GROUP):
        pg = pg + p[r * t_new:(r + 1) * t_new]
    score, _ = _block_scores(pg, ov_ref[...], q_pos[0:t_new], n_blocks)
    _, picks = _top_blocks(score)
    lane = _iota((t_new, SEL_TOPK), 1)
    idx = jnp.zeros((t_new, SEL_TOPK), jnp.int32)
    for k, pick in enumerate(picks):
        idx = jnp.where(lane == k, pick.astype(jnp.int32), idx)
    idx_ref[...] = idx


def nsa_cmp_sample(q_rows, kc, vc, slopes, overlap, *, t_new, past_len, n_blocks):
    b, g, rows, d = q_rows.shape
    n = kc.shape[2]
    spec = lambda r: pl.BlockSpec((None, None, r, d), lambda bi, gi: (bi, gi, 0, 0))
    return pl.pallas_call(
        functools.partial(_nsa_cmp_sample_body, t_new=t_new, past_len=past_len, n_blocks=n_blocks),
        grid=(b, g),
        in_specs=[pl.BlockSpec(memory_space=pltpu.SMEM), spec(rows), spec(n), spec(n),
                  pl.BlockSpec(overlap.shape, lambda bi, gi: (0, 0), pipeline_mode=pl.Buffered(1))],
        out_specs=[spec(rows), pl.BlockSpec((None, None, t_new, SEL_TOPK), lambda bi, gi: (bi, gi, 0, 0))],
        out_shape=[jax.ShapeDtypeStruct((b, g, rows, d), F32), jax.ShapeDtypeStruct((b, g, t_new, SEL_TOPK), jnp.int32)],
        compiler_params=_params("parallel", "parallel"),
        name="nsa_cmp_sample",
    )(slopes, q_rows, kc, vc, overlap)


def _nsa_selwin_sample_body(idx_ref, pt_ref, slope_ref, q_ref, oc_ref, gl_ref, kwc_ref, vwc_ref, kwn_ref, vwn_ref,
                            pool_ref, new_ref, o_ref, kbuf, vbuf, kwin, vwin, sem,
                            *, t_new, past_len, n_pages):
    bi = pl.program_id(0)
    g = pl.program_id(1)
    n_slots = t_new * SEL_TOPK
    bpp = PAGE_SIZE // SEL_BLOCK
    past_blocks = n_pages * bpp
    col_k = pl.multiple_of(g * HEAD_DIM, HEAD_DIM)
    col_v = pl.multiple_of((NSA_KV_HEADS + g) * HEAD_DIM, HEAD_DIM)

    def block_copies(slot):
        j = idx_ref[(bi * NSA_KV_HEADS + g) * n_slots + slot]
        jo = jnp.minimum(j, past_blocks - 1)
        phys = pt_ref[bi * n_pages + jo // bpp]
        off = pl.multiple_of((jo % bpp) * SEL_BLOCK, SEL_BLOCK)
        old = [pltpu.make_async_copy(pool_ref.at[phys, pl.ds(off, SEL_BLOCK), pl.ds(col, HEAD_DIM)], buf.at[slot], sem)
               for col, buf in ((col_k, kbuf), (col_v, vbuf))]
        new = [pltpu.make_async_copy(new_ref.at[bi, :, pl.ds(col, HEAD_DIM)], buf.at[slot], sem)
               for col, buf in ((col_k, kbuf), (col_v, vbuf))]
        return j < past_blocks, old, new

    def start(slot, carry):
        is_old, old, new = block_copies(slot)

        @pl.when(is_old)
        def _():
            for c in old:
                c.start()

        @pl.when(jnp.logical_not(is_old))
        def _():
            for c in new:
                c.start()
        return carry

    def wait(slot, carry):
        is_old, old, new = block_copies(slot)

        @pl.when(is_old)
        def _():
            for c in old:
                c.wait()

        @pl.when(jnp.logical_not(is_old))
        def _():
            for c in new:
                c.wait()
        return carry

    lax.fori_loop(0, n_slots, start, 0)

    q = q_ref[...]
    rows = q.shape[0]
    q_pos, slope = _sample_rows(t_new, past_len, slope_ref, g)

    wc = kwc_ref.shape[0]
    n_win = kwin.shape[0]
    kwin[0:wc, :] = kwc_ref[...]
    vwin[0:wc, :] = vwc_ref[...]
    kwin[wc:n_win, :] = jnp.zeros((n_win - wc, HEAD_DIM), F32)
    vwin[wc:n_win, :] = jnp.zeros((n_win - wc, HEAD_DIM), F32)
    kwin[wc:wc + t_new, :] = kwn_ref[...]
    vwin[wc:wc + t_new, :] = vwn_ref[...]
    dist = q_pos - (past_len - wc + _iota((rows, n_win), 1))
    mask = (dist >= 0) & (dist < WINDOW) & (_iota((rows, n_win), 1) < wc + t_new)
    p = _softmax_rows(_dot_nt(q, kwin[...].astype(BF)) * ATT_SCALE - slope * dist.astype(F32), mask)
    o_win = _dot(p.astype(BF), vwin[...].astype(BF))

    lax.fori_loop(0, n_slots, wait, 0)

    n_keys = SEL_TOPK * SEL_BLOCK
    lane = _iota((1, n_keys), 1)
    row_t = _mod(_iota((rows, 1), 0), t_new)
    o_sel = jnp.zeros((rows, HEAD_DIM), F32)
    for t in range(t_new):
        k = kbuf[t * SEL_TOPK:(t + 1) * SEL_TOPK].reshape(n_keys, HEAD_DIM).astype(BF)
        v = vbuf[t * SEL_TOPK:(t + 1) * SEL_TOPK].reshape(n_keys, HEAD_DIM).astype(BF)
        blk = jnp.zeros((1, n_keys), jnp.int32)
        for kk in range(SEL_TOPK):
            j = idx_ref[(bi * NSA_KV_HEADS + g) * n_slots + t * SEL_TOPK + kk]
            blk = jnp.where(_div(lane, SEL_BLOCK) == kk, j, blk)
        dist = q_pos - (blk * SEL_BLOCK + _mod(lane, SEL_BLOCK))
        p = _softmax_rows(_dot_nt(q, k) * ATT_SCALE - slope * dist.astype(F32), dist >= 0)
        o_sel = jnp.where(row_t == t, _dot(p.astype(BF), v), o_sel)

    gates = jax.nn.sigmoid(gl_ref[...])
    o_ref[...] = gates[:, 0:1] * oc_ref[...] + gates[:, 1:2] * o_sel + gates[:, 2:3] * o_win


def nsa_selwin_sample(idx, page_table, slopes, q_rows, o_cmp, gate_rows, win_cache, win_new, pool, new_block,
                      *, t_new, past_len):
    b, g, rows, d = q_rows.shape
    n_pages = page_table.shape[1]
    wc = win_cache.shape[1]
    n_win = -(-(wc + t_new) // LANES) * LANES
    n_slots = t_new * SEL_TOPK
    rspec = lambda last: pl.BlockSpec((None, None, rows, last), lambda bi, gi, *_: (bi, gi, 0, 0))
    kv = lambda n, slot: pl.BlockSpec((None, n, d), lambda bi, gi, *_: (bi, 0, slot * NSA_KV_HEADS + gi))
    return pl.pallas_call(
        functools.partial(_nsa_selwin_sample_body, t_new=t_new, past_len=past_len, n_pages=n_pages),
        grid_spec=pltpu.PrefetchScalarGridSpec(
            num_scalar_prefetch=2,
            grid=(b, g),
            in_specs=[pl.BlockSpec(memory_space=pltpu.SMEM), rspec(d), rspec(d), rspec(3),
                      kv(wc, 0), kv(wc, 1), kv(t_new, 0), kv(t_new, 1),
                      pl.BlockSpec(memory_space=pl.ANY), pl.BlockSpec(memory_space=pl.ANY)],
            out_specs=rspec(d),
            scratch_shapes=[pltpu.VMEM((n_slots, SEL_BLOCK, d), F32), pltpu.VMEM((n_slots, SEL_BLOCK, d), F32),
                            pltpu.VMEM((n_win, d), F32), pltpu.VMEM((n_win, d), F32),
                            pltpu.SemaphoreType.DMA(())],
        ),
        out_shape=jax.ShapeDtypeStruct((b, g, rows, d), F32),
        compiler_params=_params("arbitrary", "arbitrary"),
        name="nsa_selwin_sample",
    )(idx.reshape(-1), page_table.reshape(-1), slopes, q_rows, o_cmp, gate_rows,
      win_cache, win_cache, win_new, win_new, pool, new_block)


def _norm_proj_body(x_ref, g_ref, w_ref, hg_ref, o_ref, *, n_norm):
    h = _rms(x_ref[...], g_ref[...]).astype(BF)
    for c in range(w_ref.shape[1] // HEAD_DIM):
        z = _dot(h, w_ref[:, c * HEAD_DIM:(c + 1) * HEAD_DIM])
        if c < n_norm:
            z = _rms(z, hg_ref[...])
        o_ref[:, c * HEAD_DIM:(c + 1) * HEAD_DIM] = z


def norm_proj(x2d, g, w, head_gain, *, n_norm, tm):
    m, d = x2d.shape
    n = w.shape[1]
    return pl.pallas_call(
        functools.partial(_norm_proj_body, n_norm=n_norm),
        grid=(m // tm,),
        in_specs=[pl.BlockSpec((tm, d), lambda i: (i, 0)), _resident((1, d)), _resident(w.shape),
                  _resident((1, HEAD_DIM))],
        out_specs=pl.BlockSpec((tm, n), lambda i: (i, 0)),
        out_shape=jax.ShapeDtypeStruct((m, n), F32),
        compiler_params=_params("parallel"),
        name="norm_proj",
    )(x2d, g.reshape(1, d), w, head_gain.reshape(1, HEAD_DIM))


def _post_mix_body(x_ref, yc_ref, on_ref, mem_ref, wo1_ref, wo2_ref, mg_ref, wq_ref, qg_ref, wo_ref, o_ref):
    x = x_ref[...] + _dot(yc_ref[...], wo1_ref[...]) + _dot(on_ref[...], wo2_ref[...])
    h = _rms(x, mg_ref[...]).astype(BF)
    dm = MEM_HEADS * HEAD_DIM
    heads = []
    for hd in range(MEM_HEADS):
        sl = slice(hd * HEAD_DIM, (hd + 1) * HEAD_DIM)
        q = _rms(_dot(h, wq_ref[:, sl]), qg_ref[...]).astype(BF)
        k = mem_ref[:, sl].astype(BF)
        v = mem_ref[:, dm + hd * HEAD_DIM:dm + (hd + 1) * HEAD_DIM].astype(BF)
        s = _dot_nt(q, k) * ATT_SCALE
        e = jnp.exp(s - jnp.max(s, axis=-1, keepdims=True))
        p = e * (1.0 / jnp.sum(e, axis=-1, keepdims=True))
        heads.append(_dot(p.astype(BF), v).astype(BF))
    o_ref[...] = x + _dot(jnp.concatenate(heads, axis=1), wo_ref[...])


def post_mix(x2d, y_conv, o_nsa, mem_kv, w_out_conv, w_out_nsa, mem_g, w_q, q_g, w_o, *, tm, rows_per_batch):
    m, d = x2d.shape
    n_mem, dkv = mem_kv.shape[1:]
    per = rows_per_batch // tm
    row = lambda n: pl.BlockSpec((tm, n), lambda i: (i, 0))
    return pl.pallas_call(
        _post_mix_body,
        grid=(m // tm,),
        in_specs=[row(d), row(y_conv.shape[1]), row(o_nsa.shape[1]),
                  pl.BlockSpec((None, n_mem, dkv), lambda i: (i // per, 0, 0)),
                  _resident(w_out_conv.shape), _resident(w_out_nsa.shape), _resident((1, d)),
                  _resident(w_q.shape), _resident((1, HEAD_DIM)), _resident(w_o.shape)],
        out_specs=row(d),
        out_shape=jax.ShapeDtypeStruct((m, d), F32),
        compiler_params=_params("parallel"),
        name="post_mix",
    )(x2d, y_conv, o_nsa, mem_kv, w_out_conv, w_out_nsa, mem_g.reshape(1, d), w_q, q_g.reshape(1, HEAD_DIM), w_o)


def _weights(ffn1_w1, ffn1_w3, ffn1_w2, w_in, w_out, cmp_k_w1, cmp_k_w2, cmp_v_w1, cmp_v_w2,
             w_mem_q, w_mem_kv, w_mem_o, ffn2_w1, ffn2_w3, ffn2_w2):
    n_main = 2 * D_CONV + D_Q + 3 * D_KV
    n_gate = 3 * NSA_GROUP
    w_gate = w_in[:, n_main:].reshape(D_MODEL, 3, NSA_KV_HEADS, NSA_GROUP)
    w_gate = jnp.moveaxis(w_gate, 2, 1).reshape(D_MODEL, NSA_KV_HEADS, n_gate)
    w_gate = jnp.pad(w_gate, ((0, 0), (0, 0), (0, LANES - n_gate))).reshape(D_MODEL, NSA_KV_HEADS * LANES)
    flat = lambda w: w.reshape(CMP_BLOCK * HEAD_DIM, HEAD_DIM).astype(BF)
    return dict(
        ffn1=(ffn1_w1.astype(BF), ffn1_w3.astype(BF), ffn1_w2.astype(BF)),
        ffn2=(ffn2_w1.astype(BF), ffn2_w3.astype(BF), ffn2_w2.astype(BF)),
        w_main=w_in[:, :n_main].astype(BF), w_gate=w_gate.astype(BF),
        w_out_conv=w_out[:D_CONV].astype(BF), w_out_nsa=w_out[D_CONV:].astype(BF),
        w1k=flat(cmp_k_w1), w1v=flat(cmp_v_w1), w2k=cmp_k_w2.astype(BF), w2v=cmp_v_w2.astype(BF),
        w_mem_q=w_mem_q.astype(BF), w_mem_kv=w_mem_kv.astype(BF), w_mem_o=w_mem_o.astype(BF))


def _alibi_slopes():
    return jnp.exp2(-8.0 * jnp.arange(1, NSA_HEADS + 1, dtype=F32) / NSA_HEADS)


def _pe_rows(pe):
    return jnp.broadcast_to(pe.reshape(1, CMP_BLOCK * HEAD_DIM), (8, CMP_BLOCK * HEAD_DIM))


def _mix_front(x, W, ffn1_norm, mix_norm, head_gains, conv_left, conv_w, conv_b, conv_ln_g, conv_ln_b, *, tm, tt):
    b, t, d = x.shape
    x1 = ffn_half(x.reshape(b * t, d), ffn1_norm, *W["ffn1"], tm=tm)
    u, q, kv_cmp, kv_sel, kv_win, glog = in_proj(x1, mix_norm, W["w_main"], W["w_gate"], head_gains, tm=tm)
    u = u.reshape(b, t, D_CONV)
    y_conv = conv_module(u, conv_left, conv_w, conv_b, conv_ln_g, conv_ln_b, tt=tt)
    r3 = lambda a: a.reshape(b, t, a.shape[-1])
    return x1, u, y_conv.reshape(b * t, D_CONV), r3(q), r3(kv_cmp), r3(kv_sel), r3(kv_win), r3(glog)


def _kv_out(a):
    return a.reshape(a.shape[0], a.shape[1], 2, NSA_KV_HEADS, HEAD_DIM)


def kernel(x_prompt, x_sample, cache_kv_cmp, cache_kv_sel, cache_kv_win, cache_mem_kv, state_conv,
           page_table, mem_prompt, ffn1_norm, ffn1_w1, ffn1_w3, ffn1_w2, mix_norm, w_in, w_out,
           conv_w, conv_b, conv_ln_g, conv_ln_b, q_norm, k_norm_cmp, k_norm_sel, k_norm_win,
           cmp_k_pe, cmp_k_w1, cmp_k_w2, cmp_v_pe, cmp_v_w1, cmp_v_w2, mem_norm, mem_src_norm,
           w_mem_q, w_mem_kv, mem_q_norm, mem_k_norm, w_mem_o, ffn2_norm, ffn2_w1, ffn2_w3, ffn2_w2):
    assert ffn1_norm.shape[0] == 1, "single layer"
    L = 0
    W = _weights(ffn1_w1[L], ffn1_w3[L], ffn1_w2[L], w_in[L], w_out[L], cmp_k_w1[L], cmp_k_w2[L], cmp_v_w1[L],
                 cmp_v_w2[L], w_mem_q[L], w_mem_kv[L], w_mem_o[L], ffn2_w1[L], ffn2_w3[L], ffn2_w2[L])
    head_gains = jnp.pad(jnp.stack([q_norm[L], k_norm_sel[L], k_norm_win[L]]), ((0, 5), (0, 0)))
    slopes = _alibi_slopes()
    pe_k, pe_v = _pe_rows(cmp_k_pe[L]), _pe_rows(cmp_v_pe[L])
    conv_args = (conv_w[L], conv_b[L], conv_ln_g[L], conv_ln_b[L])
    dm = MEM_HEADS * HEAD_DIM

    def tail(x1, y_conv, o_nsa, mem_kv, tm_mix, tm_ffn, rows_per_batch):
        x3 = post_mix(x1, y_conv, o_nsa, mem_kv, W["w_out_conv"], W["w_out_nsa"], mem_norm[L], W["w_mem_q"],
                      mem_q_norm[L], W["w_mem_o"], tm=tm_mix, rows_per_batch=rows_per_batch)
        return ffn_half(x3, ffn2_norm[L], *W["ffn2"], tm=tm_ffn)

    bp, tp, d = x_prompt.shape
    tm_p = 512 if (bp * tp) % 512 == 0 else 256
    left0 = jnp.zeros((bp, CONV_HALO, D_CONV), F32)
    x1, u, y_conv, q, kv_cmp, kv_sel, kv_win, glog = _mix_front(
        x_prompt, W, ffn1_norm[L], mix_norm[L], head_gains, left0, *conv_args, tm=tm_p, tt=256)
    n_chunks = tp // CMP_STRIDE
    terms = cmp_terms_dense(kv_cmp.reshape(bp, n_chunks, CHUNK_LANES), W["w1k"], W["w1v"], rows=n_chunks)
    kc, vc = cmp_combine(terms, jnp.zeros((bp, 8, 2 * D_KV), F32), pe_k, pe_v, W["w1k"], W["w1v"], W["w2k"],
                         W["w2v"], k_norm_cmp[L])
    n_blocks = tp // SEL_BLOCK
    ov_p = _overlap_matrix(n_chunks, n_chunks - 1, max(n_blocks, 8), n_blocks)
    o_nsa = nsa_prompt(q, kc, vc, kv_sel, kv_win, glog, slopes, ov_p, tq=256)
    n_mem = mem_prompt.shape[1]
    mem_kv_p = norm_proj(mem_prompt.reshape(bp * n_mem, d), mem_src_norm[L], W["w_mem_kv"], mem_k_norm[L],
                         n_norm=MEM_HEADS, tm=256).reshape(bp, n_mem, 2 * dm)
    y_prompt = tail(x1, y_conv, o_nsa.reshape(bp * tp, D_Q), mem_kv_p, tm_p, tm_p, tp).reshape(bp, tp, d)
    wlen = min(WINDOW, tp)
    outs_p = (_kv_out(kv_cmp), _kv_out(kv_sel), _kv_out(kv_win[:, tp - wlen:]),
              mem_kv_p.reshape(bp, n_mem, 2, MEM_HEADS, HEAD_DIM), u[:, tp - (CONV_WIDTH - 1):])

    bs, ts, _ = x_sample.shape
    n_pages = page_table.shape[1]
    past_len = n_pages * PAGE_SIZE
    state = state_conv[L]
    left = jnp.pad(state, ((0, 0), (CONV_HALO - (CONV_WIDTH - 1), 0), (0, 0)))
    x1, u, y_conv, q, kv_cmp, kv_sel, kv_win, glog = _mix_front(
        x_sample, W, ffn1_norm[L], mix_norm[L], head_gains, left, *conv_args, tm=bs * ts, tt=ts)
    cpp = PAGE_SIZE // CMP_STRIDE
    pool_cmp = cache_kv_cmp[L].reshape(-1, cpp, CHUNK_LANES)
    terms = cmp_terms_paged(pool_cmp, page_table, W["w1k"], W["w1v"])
    new_chunk = jnp.pad(kv_cmp, ((0, 0), (0, CMP_STRIDE - ts), (0, 0))).reshape(1, bs, CHUNK_LANES)
    terms_new = cmp_terms_dense(new_chunk, W["w1k"], W["w1v"], rows=bs).reshape(bs, 1, 2 * D_KV)
    kc, vc = cmp_combine(terms, jnp.pad(terms_new, ((0, 0), (0, 7), (0, 0))), pe_k, pe_v, W["w1k"], W["w1v"],
                         W["w2k"], W["w2v"], k_norm_cmp[L])
    n_cmp = past_len // CMP_STRIDE
    n_blocks = past_len // SEL_BLOCK + 1
    n_blk_pad = -(-n_blocks // LANES) * LANES
    ov_s = _overlap_matrix(n_cmp, n_cmp, n_blk_pad, n_blocks)
    rows = NSA_GROUP * ts
    to_rows = lambda a: jnp.transpose(a.reshape(bs, ts, NSA_KV_HEADS, NSA_GROUP, -1), (0, 2, 3, 1, 4)).reshape(
        bs, NSA_KV_HEADS, rows, a.shape[-1] // NSA_HEADS)
    q_rows = to_rows(q)
    o_cmp, idx = nsa_cmp_sample(q_rows, kc, vc, slopes, ov_s, t_new=ts, past_len=past_len, n_blocks=n_blocks)
    gl = glog.reshape(bs, ts, NSA_KV_HEADS, LANES)[..., :3 * NSA_GROUP].reshape(bs, ts, NSA_KV_HEADS, 3, NSA_GROUP)
    gate_rows = jnp.transpose(gl, (0, 2, 4, 1, 3)).reshape(bs, NSA_KV_HEADS, rows, 3)
    new_block = jnp.pad(kv_sel, ((0, 0), (0, SEL_BLOCK - ts), (0, 0)))
    o_rows = nsa_selwin_sample(idx, page_table, slopes, q_rows, o_cmp, gate_rows,
                               cache_kv_win[L].reshape(bs, -1, D_KV), kv_win,
                               cache_kv_sel[L].reshape(-1, PAGE_SIZE, D_KV), new_block, t_new=ts, past_len=past_len)
    o_nsa = jnp.transpose(o_rows.reshape(bs, NSA_KV_HEADS, NSA_GROUP, ts, HEAD_DIM), (0, 3, 1, 2, 4))
    o_nsa = o_nsa.reshape(bs * ts, D_Q).astype(BF)
    mem_kv_s = cache_mem_kv[L].reshape(bs, -1, 2 * dm)
    y_sample = tail(x1, y_conv, o_nsa, mem_kv_s, ts, bs * ts, ts).reshape(bs, ts, d)
    win_all = jnp.concatenate([cache_kv_win[L], _kv_out(kv_win)], axis=1)
    conv_all = jnp.concatenate([state, u], axis=1)
    outs_s = (_kv_out(kv_cmp), _kv_out(kv_sel), win_all[:, ts:], conv_all[:, ts:])

    st = lambda a: a[None]
    return (y_prompt, y_sample) + tuple(st(a) for a in outs_p) + tuple(st(a) for a in outs_s)
```

```python
import functools

import jax
import jax.numpy as jnp
import numpy as np
from jax import lax
from jax.experimental import pallas as pl
from jax.experimental.pallas import tpu as pltpu

D_MODEL = 2048
PAGE_SIZE = 128
D_CONV = D_MODEL // 2
CONV_WIDTH = 31
NSA_HEADS = 8
HEAD_DIM = 128
NSA_KV_HEADS = 2
NSA_GROUP = NSA_HEADS // NSA_KV_HEADS
CMP_BLOCK = 32
CMP_STRIDE = 16
SEL_BLOCK = 64
SEL_TOPK = 16
WINDOW = 512
MEM_HEADS = 4
NORM_EPS = 1e-6
NEG_INF = -1e30
PICKED = -3e38
MAX_FLOOR = -1e29
LOG2E = 1.4426950408889634
FORCE_BONUS = 1e4
D_Q = NSA_HEADS * HEAD_DIM
D_KV = 2 * NSA_KV_HEADS * HEAD_DIM
CHUNK_LANES = CMP_STRIDE * D_KV
ATT_SCALE = HEAD_DIM ** -0.5
LANES = 128
CONV_HALO = 32

VMEM_LIMIT_BYTES = 56 * 1024 * 1024
BF = jnp.bfloat16
F32 = jnp.float32


def _params(*sem):
    return pltpu.CompilerParams(dimension_semantics=sem, vmem_limit_bytes=VMEM_LIMIT_BYTES)


def _resident(shape):
    nd = len(shape)
    return pl.BlockSpec(shape, lambda *_: (0,) * nd, pipeline_mode=pl.Buffered(1))


def _rms(x, g):
    return x * lax.rsqrt(jnp.mean(x * x, axis=-1, keepdims=True) + NORM_EPS) * g


def _dot(a, b):
    return jnp.dot(a, b, preferred_element_type=F32)


def _dot_nt(a, b):
    return lax.dot_general(a, b, (((1,), (1,)), ((), ())), preferred_element_type=F32)


def _iota(shape, dim):
    return lax.broadcasted_iota(jnp.int32, shape, dim)


def _div(x, n):
    return x >> (n.bit_length() - 1)


def _mod(x, n):
    return x & (n - 1)


def _split3(x):
    hi = x.astype(BF)
    r1 = x - hi.astype(F32)
    mid = r1.astype(BF)
    lo = (r1 - mid.astype(F32)).astype(BF)
    return hi, mid, lo


def _ffn_body(x_ref, g_ref, w1_ref, w3_ref, w2_ref, o_ref, h_ref, *, n_f):
    f = pl.program_id(1)

    @pl.when(f == 0)
    def _():
        h_ref[...] = _rms(x_ref[...], g_ref[...]).astype(BF)
        o_ref[...] = jnp.zeros_like(o_ref)

    h = h_ref[...]
    a = _dot(h, w1_ref[...])
    b = _dot(h, w3_ref[...])
    act = (a * jax.nn.sigmoid(a) * b).astype(BF)
    o_ref[...] += _dot(act, w2_ref[...])

    @pl.when(f == n_f - 1)
    def _():
        o_ref[...] = x_ref[...] + 0.5 * o_ref[...]


def ffn_half(x2d, g, w1, w3, w2, *, tm, tf=512):
    m, d = x2d.shape
    n_f = w1.shape[1] // tf
    return pl.pallas_call(
        functools.partial(_ffn_body, n_f=n_f),
        grid=(m // tm, n_f),
        in_specs=[
            pl.BlockSpec((tm, d), lambda i, f: (i, 0)),
            pl.BlockSpec((1, d), lambda i, f: (0, 0)),
            pl.BlockSpec((d, tf), lambda i, f: (0, f)),
            pl.BlockSpec((d, tf), lambda i, f: (0, f)),
            pl.BlockSpec((tf, d), lambda i, f: (f, 0)),
        ],
        out_specs=pl.BlockSpec((tm, d), lambda i, f: (i, 0)),
        out_shape=jax.ShapeDtypeStruct((m, d), F32),
        scratch_shapes=[pltpu.VMEM((tm, d), BF)],
        compiler_params=_params("parallel", "arbitrary"),
        name="ffn_half",
    )(x2d, g.reshape(1, d), w1, w3, w2)


def _in_proj_body(x_ref, g_ref, w_ref, wg_ref, hg_ref, u_ref, q_ref, kc_ref, ks_ref, kw_ref, gl_ref):
    h = _rms(x_ref[...], g_ref[...]).astype(BF)
    a = _dot(h, w_ref[:, 0:D_CONV])
    gate = _dot(h, w_ref[:, D_CONV:2 * D_CONV])
    u_ref[...] = a * jax.nn.sigmoid(gate)
    o = 2 * D_CONV
    for hd in range(NSA_HEADS):
        z = _dot(h, w_ref[:, o + hd * HEAD_DIM:o + (hd + 1) * HEAD_DIM])
        q_ref[:, hd * HEAD_DIM:(hd + 1) * HEAD_DIM] = _rms(z, hg_ref[0:1, :]).astype(BF)
    o += D_Q
    kc_ref[...] = _dot(h, w_ref[:, o:o + D_KV])
    for n, ref in ((1, ks_ref), (2, kw_ref)):
        o += D_KV
        for c in range(2 * NSA_KV_HEADS):
            z = _dot(h, w_ref[:, o + c * HEAD_DIM:o + (c + 1) * HEAD_DIM])
            if c < NSA_KV_HEADS:
                z = _rms(z, hg_ref[n:n + 1, :])
            ref[:, c * HEAD_DIM:(c + 1) * HEAD_DIM] = z
    gl_ref[...] = _dot(h, wg_ref[...])


def in_proj(x2d, g, w_main, w_gate, head_gains, *, tm):
    m, d = x2d.shape
    row = lambda n: pl.BlockSpec((tm, n), lambda i: (i, 0))
    return pl.pallas_call(
        _in_proj_body,
        grid=(m // tm,),
        in_specs=[row(d), _resident((1, d)), _resident(w_main.shape), _resident(w_gate.shape),
                  _resident(head_gains.shape)],
        out_specs=[row(D_CONV), row(D_Q), row(D_KV), row(D_KV), row(D_KV), row(2 * LANES)],
        out_shape=[jax.ShapeDtypeStruct((m, D_CONV), F32), jax.ShapeDtypeStruct((m, D_Q), BF),
                   jax.ShapeDtypeStruct((m, D_KV), F32), jax.ShapeDtypeStruct((m, D_KV), F32),
                   jax.ShapeDtypeStruct((m, D_KV), F32), jax.ShapeDtypeStruct((m, 2 * LANES), F32)],
        compiler_params=_params("parallel"),
        name="in_proj",
    )(x2d, g.reshape(1, d), w_main, w_gate, head_gains)


CONV_ROWS = 32


def _conv_body(u_ref, halo_ref, left_ref, w_ref, b_ref, lg_ref, lb_ref, y_ref, win_ref, *, tt, tiled):
    i = pl.program_id(1)

    @pl.when(i == 0)
    def _():
        win_ref[0:CONV_HALO, :] = left_ref[...]

    if tiled:
        @pl.when(i > 0)
        def _():
            win_ref[0:CONV_HALO, :] = halo_ref[...]

    win_ref[CONV_HALO:CONV_HALO + tt, :] = u_ref[...]
    first = CONV_HALO - (CONV_WIDTH - 1)
    for r0 in range(0, tt, CONV_ROWS):
        rows = min(CONV_ROWS, tt - r0)
        acc = jnp.broadcast_to(b_ref[...], (rows, D_CONV))
        for k in range(CONV_WIDTH):
            acc = acc + w_ref[k:k + 1, :] * win_ref[r0 + first + k:r0 + first + k + rows, :]
        xc = acc - jnp.mean(acc, axis=-1, keepdims=True)
        yn = xc * lax.rsqrt(jnp.mean(xc * xc, axis=-1, keepdims=True) + NORM_EPS) * lg_ref[...] + lb_ref[...]
        y_ref[r0:r0 + rows, :] = (yn * jax.nn.sigmoid(yn)).astype(BF)


def conv_module(u, left, conv_w, conv_b, ln_g, ln_b, *, tt):
    b, t, c = u.shape
    hb = tt // CONV_HALO if tt >= CONV_HALO else 1
    halo_rows = min(CONV_HALO, t)
    w_pad = jnp.pad(conv_w, ((0, CONV_HALO - CONV_WIDTH), (0, 0)))
    return pl.pallas_call(
        functools.partial(_conv_body, tt=tt, tiled=t > tt),
        grid=(b, t // tt),
        in_specs=[
            pl.BlockSpec((None, tt, c), lambda bi, i: (bi, i, 0)),
            pl.BlockSpec((None, halo_rows, c), lambda bi, i: (bi, jnp.maximum(i * hb - 1, 0), 0)),
            pl.BlockSpec((None, CONV_HALO, c), lambda bi, i: (bi, 0, 0)),
            _resident(w_pad.shape), _resident((1, c)), _resident((1, c)), _resident((1, c)),
        ],
        out_specs=pl.BlockSpec((None, tt, c), lambda bi, i: (bi, i, 0)),
        out_shape=jax.ShapeDtypeStruct((b, t, c), BF),
        scratch_shapes=[pltpu.VMEM((CONV_HALO + tt, c), F32)],
        compiler_params=_params("parallel", "arbitrary"),
        name="conv_module",
    )(u, u, left, w_pad, conv_b.reshape(1, c), ln_g.reshape(1, c), ln_b.reshape(1, c))


def _cmp_terms_body(*refs, n_src, paged):
    n_prefetch = 1 if paged else 0
    src = refs[n_prefetch:n_prefetch + n_src]
    wk_ref, wv_ref, o_ref = refs[n_prefetch + n_src:]
    half = CMP_STRIDE * HEAD_DIM
    n_c = 2 * NSA_KV_HEADS
    cpp = PAGE_SIZE // CMP_STRIDE
    for c in range(n_c):
        if paged:
            cols = [jnp.concatenate([s[pl.ds(j * n_c + c, cpp, stride=CMP_STRIDE * n_c), :] for s in src], axis=0)
                    for j in range(CMP_STRIDE)]
        else:
            cols = [src[0][:, j * D_KV + c * HEAD_DIM:j * D_KV + (c + 1) * HEAD_DIM] for j in range(CMP_STRIDE)]
        x = jnp.concatenate(cols, axis=1).astype(BF)
        w_ref = wk_ref if c < NSA_KV_HEADS else wv_ref
        for r in range(2):
            o_ref[:, (2 * c + r) * HEAD_DIM:(2 * c + r + 1) * HEAD_DIM] = _dot(x, w_ref[r * half:(r + 1) * half, :])


def cmp_terms_dense(x, w1k, w1v, *, rows):
    b, n, l = x.shape
    return pl.pallas_call(
        functools.partial(_cmp_terms_body, n_src=1, paged=False),
        grid=(b, n // rows),
        in_specs=[pl.BlockSpec((None, rows, l), lambda bi, i: (bi, i, 0)),
                  _resident(w1k.shape), _resident(w1v.shape)],
        out_specs=pl.BlockSpec((None, rows, 2 * D_KV), lambda bi, i: (bi, i, 0)),
        out_shape=jax.ShapeDtypeStruct((b, n, 2 * D_KV), F32),
        compiler_params=_params("parallel", "parallel"),
        name="cmp_terms_dense",
    )(x, w1k, w1v)


CMP_PAGES_PER_STEP = 32
POOL_ROWS_PER_TOKEN = D_KV // HEAD_DIM
POOL_ROWS_PER_PAGE = PAGE_SIZE * POOL_ROWS_PER_TOKEN


def cmp_terms_paged(pool_rows, page_table, w1k, w1v):
    b, n_pages = page_table.shape
    pps = min(CMP_PAGES_PER_STEP, n_pages)
    cpp = PAGE_SIZE // CMP_STRIDE
    rows = pps * cpp

    def page_spec(k):
        return pl.BlockSpec((POOL_ROWS_PER_PAGE, HEAD_DIM), lambda bi, i, pt: (pt[bi, i * pps + k], 0))

    return pl.pallas_call(
        functools.partial(_cmp_terms_body, n_src=pps, paged=True),
        grid_spec=pltpu.PrefetchScalarGridSpec(
            num_scalar_prefetch=1,
            grid=(b, n_pages // pps),
            in_specs=[page_spec(k) for k in range(pps)]
            + [pl.BlockSpec(w1k.shape, lambda bi, i, pt: (0, 0), pipeline_mode=pl.Buffered(1)),
               pl.BlockSpec(w1v.shape, lambda bi, i, pt: (0, 0), pipeline_mode=pl.Buffered(1))],
            out_specs=pl.BlockSpec((None, rows, 2 * D_KV), lambda bi, i, pt: (bi, i, 0)),
        ),
        out_shape=jax.ShapeDtypeStruct((b, n_pages * cpp, 2 * D_KV), F32),
        compiler_params=_params("parallel", "parallel"),
        name="cmp_terms_paged",
    )(page_table, *([pool_rows] * pps), w1k, w1v)


def _cmp_combine_body(u_ref, x_ref, pek_ref, pev_ref, wk1_ref, wv1_ref, wk2_ref, wv2_ref, kg_ref, kc_ref, vc_ref,
                      *, n, v_transposed):
    last = _iota((n, 1), 0) == n - 1
    for c in range(2 * NSA_KV_HEADS):
        is_k = c < NSA_KV_HEADS
        pe_ref, w1_ref, w2_ref = (pek_ref, wk1_ref, wk2_ref) if is_k else (pev_ref, wv1_ref, wv2_ref)
        h0 = _dot(pe_ref[...].astype(BF), w1_ref[...])[0:1, :]
        u0 = u_ref[:, (2 * c) * HEAD_DIM:(2 * c + 1) * HEAD_DIM]
        u1 = u_ref[:, (2 * c + 1) * HEAD_DIM:(2 * c + 2) * HEAD_DIM]
        nxt = x_ref[0:1, (2 * c + 1) * HEAD_DIM:(2 * c + 2) * HEAD_DIM]
        u1 = jnp.where(last, nxt, pltpu.roll(u1, n - 1, axis=0))
        hid = h0 + u0 + u1
        z = _dot((hid * jax.nn.sigmoid(hid)).astype(BF), w2_ref[...])
        g = c % NSA_KV_HEADS
        if is_k:
            kc_ref[g] = _rms(z, kg_ref[...]).astype(BF)
        else:
            vc_ref[g] = (z.T if v_transposed else z).astype(BF)


def cmp_combine(u, u_next, pe_k, pe_v, w1k, w1v, w2k, w2v, k_gain, *, v_transposed):
    b, n, l = u.shape
    out = jax.ShapeDtypeStruct((b, NSA_KV_HEADS, n, HEAD_DIM), BF)
    ospec = pl.BlockSpec((None, NSA_KV_HEADS, n, HEAD_DIM), lambda bi: (bi, 0, 0, 0))
    out_v = jax.ShapeDtypeStruct((b, NSA_KV_HEADS, HEAD_DIM, n), BF) if v_transposed else out
    ospec_v = pl.BlockSpec((None, NSA_KV_HEADS, HEAD_DIM, n), lambda bi: (bi, 0, 0, 0)) if v_transposed else ospec
    return pl.pallas_call(
        functools.partial(_cmp_combine_body, n=n, v_transposed=v_transposed),
        grid=(b,),
        in_specs=[pl.BlockSpec((None, n, l), lambda bi: (bi, 0, 0)),
                  pl.BlockSpec((None, 8, l), lambda bi: (bi, 0, 0)),
                  _resident(pe_k.shape), _resident(pe_v.shape), _resident(w1k.shape), _resident(w1v.shape),
                  _resident(w2k.shape), _resident(w2v.shape), _resident((1, HEAD_DIM))],
        out_specs=[ospec, ospec_v],
        out_shape=[out, out_v],
        compiler_params=_params("parallel"),
        name="cmp_combine",
    )(u, u_next, pe_k, pe_v, w1k, w1v, w2k, w2v, k_gain.reshape(1, HEAD_DIM))


def _softmax_rows(s, mask):
    s = jnp.where(mask, s, NEG_INF)
    m = jnp.max(s, axis=-1, keepdims=True)
    e = jnp.where(mask, jnp.exp(s - m), 0.0)
    l = jnp.sum(e, axis=-1, keepdims=True)
    return e * jnp.where(l > 0.0, 1.0 / l, 0.0)


def _block_scores(pg, ov, q_pos, n_blocks, axis):
    hi, mid, lo = _split3(pg)
    imp = _dot(hi, ov) + _dot(mid, ov) + _dot(lo, ov) if axis == 1 else _dot(ov, hi) + _dot(ov, mid) + _dot(ov, lo)
    j = _iota(imp.shape, axis)
    jc = _div(q_pos, SEL_BLOCK)
    valid = j * SEL_BLOCK <= q_pos
    forced = (j == 0) | (j == jc) | (j == jc - 1)
    score = jnp.where(valid, imp + jnp.where(forced, FORCE_BONUS, 0.0), NEG_INF)
    return jnp.where(j < n_blocks, score, PICKED), valid


def _top_blocks(score, axis):
    j = _iota(score.shape, axis).astype(F32)
    sel = jnp.zeros(score.shape, jnp.bool_)
    picks = []
    for _ in range(SEL_TOPK):
        m = jnp.max(score, axis=axis, keepdims=True)
        idx = jnp.min(jnp.where(score == m, j, float(score.shape[axis])), axis=axis, keepdims=True)
        one = j == idx
        sel = sel | one
        score = jnp.where(one, PICKED, score)
        picks.append(idx)
    return sel, picks


def _softmax_cols(s2, mask):
    s2 = jnp.where(mask, s2, NEG_INF)
    m = jnp.maximum(jnp.max(s2, axis=0, keepdims=True), MAX_FLOOR)
    e = jnp.exp2(s2 - m)
    l = jnp.sum(e, axis=0, keepdims=True)
    return e * jnp.where(l > 0.0, 1.0 / l, 0.0)


SEL_TK = 512
WIN_TK = 256


def _nsa_prompt_body(slope_ref, q_ref, kc_ref, vct_ref, ks_ref, vs_ref, kw_ref, vw_ref, gl_ref, ovt_ref, o_ref,
                     ksb, vst, kwb, vwt, bias_ref, oc_ref, m_ref, l_ref, acc_ref, *, tq, seq, n_cmp, n_blocks):
    g = pl.program_id(1)
    qt = pl.program_id(2)
    t0 = qt * tq
    n_win = min(WINDOW + tq, seq)

    @pl.when(qt == 0)
    def _():
        ksb[...] = ks_ref[...].astype(BF)
        kwb[...] = kw_ref[...].astype(BF)
        for kt in range(seq // SEL_TK):
            vst[kt] = vs_ref[kt * SEL_TK:(kt + 1) * SEL_TK, :].T.astype(BF)
        for kt in range(seq // WIN_TK):
            vwt[kt] = vw_ref[kt * WIN_TK:(kt + 1) * WIN_TK, :].T.astype(BF)

    q_rows = [q_ref[:, r * HEAD_DIM:(r + 1) * HEAD_DIM] for r in range(NSA_GROUP)]
    slopes = [slope_ref[g * NSA_GROUP + r] * LOG2E for r in range(NSA_GROUP)]
    scale = ATT_SCALE * LOG2E
    q_pos = t0 + _iota((1, tq), 1)

    kc = kc_ref[...]
    vct = vct_ref[...]
    n_pad = kc.shape[0]
    i = _iota((n_pad, tq), 0)
    dist = q_pos - (i * CMP_STRIDE + (CMP_BLOCK - 1))
    mask = (dist >= 0) & (i < n_cmp)
    dist_f = dist.astype(F32)
    pg = jnp.zeros((n_pad, tq), F32)
    for r in range(NSA_GROUP):
        p = _softmax_cols(_dot_nt(kc, q_rows[r]) * scale - slopes[r] * dist_f, mask)
        pg = pg + p
        oc_ref[r] = _dot(vct, p.astype(BF))
    score, valid = _block_scores(pg, ovt_ref[...], q_pos, n_blocks, 0)
    sel, _ = _top_blocks(score, 0)
    sel_bf = jnp.where(sel & valid, 1.0, 0.0).astype(BF)

    nb = sel_bf.shape[0]
    key_minus_query = _iota((SEL_TK, tq), 0) - _iota((SEL_TK, tq), 1)
    for r in range(NSA_GROUP):
        bias_ref[r] = slopes[r] * key_minus_query.astype(F32)
    m_ref[...] = jnp.full(m_ref.shape, MAX_FLOOR, F32)
    l_ref[...] = jnp.zeros(l_ref.shape, F32)
    acc_ref[...] = jnp.zeros(acc_ref.shape, F32)

    def sel_tile(kt, carry):
        k0 = pl.multiple_of(kt * SEL_TK, SEL_TK)
        k = ksb[pl.ds(k0, SEL_TK), :]
        vt = vst[kt]
        expand = jnp.where(_iota((SEL_TK, nb), 1) == kt * (SEL_TK // SEL_BLOCK) + _div(_iota((SEL_TK, nb), 0), SEL_BLOCK),
                           1.0, 0.0).astype(BF)
        visible = (_dot(expand, sel_bf) > 0.5) & (key_minus_query <= t0 - k0)
        mask_bias = jnp.where(visible, 0.0, NEG_INF)
        for r in range(NSA_GROUP):
            x = _dot_nt(k, q_rows[r]) * scale + bias_ref[r] + mask_bias
            shift = slopes[r] * (t0 - k0).astype(F32)
            m_old = m_ref[r]
            m_new = jnp.maximum(m_old, jnp.max(x, axis=0, keepdims=True) - shift)
            alpha = jnp.exp2(m_old - m_new)
            p = jnp.exp2(x - (m_new + shift))
            l_ref[r] = alpha * l_ref[r] + jnp.sum(p, axis=0, keepdims=True)
            acc_ref[r] = alpha * acc_ref[r] + _dot(vt, p.astype(BF))
            m_ref[r] = m_new
        return carry

    lax.fori_loop(0, (t0 + tq + SEL_TK - 1) // SEL_TK, sel_tile, 0)

    k_lo = pl.multiple_of(jnp.maximum(t0 + tq - n_win, 0), WIN_TK)
    kw = kwb[pl.ds(k_lo, n_win), :]
    w0 = k_lo // WIN_TK
    vwin = jnp.concatenate([vwt[w0 + n] for n in range(n_win // WIN_TK)], axis=1)
    dist = q_pos - (k_lo + _iota((n_win, tq), 0))
    mask = (dist >= 0) & (dist < WINDOW)
    dist_f = dist.astype(F32)

    gates = jax.nn.sigmoid(gl_ref[...]).T
    for r in range(NSA_GROUP):
        gate = lambda c: gates[c * NSA_GROUP + r:c * NSA_GROUP + r + 1, :]
        p = _softmax_cols(_dot_nt(kw, q_rows[r]) * scale - slopes[r] * dist_f, mask)
        o_win = _dot(vwin, p.astype(BF))
        l = l_ref[r]
        o_sel = acc_ref[r] * jnp.where(l > 0.0, 1.0 / l, 0.0)
        o = gate(0) * oc_ref[r] + gate(1) * o_sel + gate(2) * o_win
        o_ref[:, r * HEAD_DIM:(r + 1) * HEAD_DIM] = o.T.astype(BF)


def nsa_prompt(q, kc, vct, kv_sel, kv_win, glog, slopes, overlap_t, *, tq):
    b, t, _ = q.shape
    n_pad = kc.shape[2]
    n_cmp = t // CMP_STRIDE - 1
    n_blocks = t // SEL_BLOCK
    gq = NSA_GROUP * HEAD_DIM
    kv = lambda slot: pl.BlockSpec((None, t, HEAD_DIM), lambda bi, g, i: (bi, 0, slot * NSA_KV_HEADS + g),
                                   pipeline_mode=pl.Buffered(1))
    stat = pltpu.VMEM((NSA_GROUP, 1, tq), F32)
    head_t = pltpu.VMEM((NSA_GROUP, HEAD_DIM, tq), F32)
    return pl.pallas_call(
        functools.partial(_nsa_prompt_body, tq=tq, seq=t, n_cmp=n_cmp, n_blocks=n_blocks),
        grid=(b, NSA_KV_HEADS, t // tq),
        in_specs=[pl.BlockSpec(memory_space=pltpu.SMEM),
                  pl.BlockSpec((None, tq, gq), lambda bi, g, i: (bi, i, g)),
                  pl.BlockSpec((None, None, n_pad, HEAD_DIM), lambda bi, g, i: (bi, g, 0, 0)),
                  pl.BlockSpec((None, None, HEAD_DIM, n_pad), lambda bi, g, i: (bi, g, 0, 0)),
                  kv(0), kv(1), kv(0), kv(1),
                  pl.BlockSpec((None, tq, LANES), lambda bi, g, i: (bi, i, g)),
                  pl.BlockSpec(overlap_t.shape, lambda bi, g, i: (0, 0), pipeline_mode=pl.Buffered(1))],
        out_specs=pl.BlockSpec((None, tq, gq), lambda bi, g, i: (bi, i, g)),
        out_shape=jax.ShapeDtypeStruct((b, t, D_Q), BF),
        scratch_shapes=[pltpu.VMEM((t, HEAD_DIM), BF), pltpu.VMEM((t // SEL_TK, HEAD_DIM, SEL_TK), BF),
                        pltpu.VMEM((t, HEAD_DIM), BF), pltpu.VMEM((t // WIN_TK, HEAD_DIM, WIN_TK), BF),
                        pltpu.VMEM((NSA_GROUP, SEL_TK, tq), F32), head_t, stat, stat, head_t],
        compiler_params=_params("parallel", "parallel", "arbitrary"),
        name="nsa_prompt",
    )(slopes, q, kc, vct, kv_sel, kv_sel, kv_win, kv_win, glog, overlap_t)


def _overlap_matrix(n_cmp_pad, n_cmp, n_blk_pad, n_blocks):
    i = np.arange(n_cmp_pad)[:, None]
    j = np.arange(n_blk_pad)[None, :]
    per = SEL_BLOCK // CMP_STRIDE
    lo = j * per - (CMP_BLOCK // CMP_STRIDE - 1)
    ov = (i >= lo) & (i < lo + per + CMP_BLOCK // CMP_STRIDE - 1) & (i < n_cmp) & (j < n_blocks)
    return jnp.asarray(ov, BF)


def _sample_rows(t_new, past_len, slope_ref, g):
    rows = NSA_GROUP * t_new
    ri = _iota((rows, 1), 0)
    q_pos = past_len + _mod(ri, t_new)
    slope = jnp.zeros((rows, 1), F32)
    for r in range(NSA_GROUP):
        slope = jnp.where(_div(ri, t_new) == r, slope_ref[g * NSA_GROUP + r], slope)
    return q_pos, slope


def _nsa_cmp_sample_body(slope_ref, q_ref, kc_ref, vc_ref, ov_ref, oc_ref, idx_ref, *, t_new, past_len, n_blocks):
    g = pl.program_id(1)
    q = q_ref[...]
    kc = kc_ref[...]
    n = kc.shape[0]
    q_pos, slope = _sample_rows(t_new, past_len, slope_ref, g)
    dist = q_pos - (_iota((q.shape[0], n), 1) * CMP_STRIDE + (CMP_BLOCK - 1))
    p = _softmax_rows(_dot_nt(q, kc) * ATT_SCALE - slope * dist.astype(F32), dist >= 0)
    oc_ref[...] = _dot(p.astype(BF), vc_ref[...])
    pg = p[0:t_new]
    for r in range(1, NSA_GROUP):
        pg = pg + p[r * t_new:(r + 1) * t_new]
    score, _ = _block_scores(pg, ov_ref[...], q_pos[0:t_new], n_blocks, 1)
    _, picks = _top_blocks(score, 1)
    lane = _iota((t_new, SEL_TOPK), 1)
    idx = jnp.zeros((t_new, SEL_TOPK), jnp.int32)
    for k, pick in enumerate(picks):
        idx = jnp.where(lane == k, pick.astype(jnp.int32), idx)
    idx_ref[...] = idx


def nsa_cmp_sample(q_rows, kc, vc, slopes, overlap, *, t_new, past_len, n_blocks):
    b, g, rows, d = q_rows.shape
    n = kc.shape[2]
    spec = lambda r: pl.BlockSpec((None, None, r, d), lambda bi, gi: (bi, gi, 0, 0))
    return pl.pallas_call(
        functools.partial(_nsa_cmp_sample_body, t_new=t_new, past_len=past_len, n_blocks=n_blocks),
        grid=(b, g),
        in_specs=[pl.BlockSpec(memory_space=pltpu.SMEM), spec(rows), spec(n), spec(n),
                  pl.BlockSpec(overlap.shape, lambda bi, gi: (0, 0), pipeline_mode=pl.Buffered(1))],
        out_specs=[spec(rows), pl.BlockSpec((None, None, t_new, SEL_TOPK), lambda bi, gi: (bi, gi, 0, 0))],
        out_shape=[jax.ShapeDtypeStruct((b, g, rows, d), F32), jax.ShapeDtypeStruct((b, g, t_new, SEL_TOPK), jnp.int32)],
        compiler_params=_params("parallel", "parallel"),
        name="nsa_cmp_sample",
    )(slopes, q_rows, kc, vc, overlap)


def _nsa_selwin_sample_body(idx_ref, pt_ref, slope_ref, q_ref, oc_ref, gl_ref, wc_ref, kwn_ref, vwn_ref,
                            pool_ref, new_ref, o_ref, kvbuf, kwin, vwin, sems,
                            *, t_new, past_len, n_pages):
    bi = pl.program_id(0)
    g = pl.program_id(1)
    n_slots = t_new * SEL_TOPK
    bpp = PAGE_SIZE // SEL_BLOCK
    past_blocks = n_pages * bpp
    rpt = POOL_ROWS_PER_TOKEN
    blk_rows = SEL_BLOCK * rpt

    def block_copy(slot):
        j = idx_ref[(bi * NSA_KV_HEADS + g) * n_slots + slot]
        jo = jnp.minimum(j, past_blocks - 1)
        phys = pt_ref[bi * n_pages + jo // bpp]
        row0 = pl.multiple_of((phys * PAGE_SIZE + (jo % bpp) * SEL_BLOCK) * rpt, blk_rows)
        dst = kvbuf.at[pl.ds(pl.multiple_of(slot * blk_rows, blk_rows), blk_rows), :]
        sem = sems.at[slot // SEL_TOPK]
        old = pltpu.make_async_copy(pool_ref.at[pl.ds(row0, blk_rows), :], dst, sem)
        new = pltpu.make_async_copy(new_ref.at[bi], dst, sem)
        return j < past_blocks, old, new

    def start(slot, carry):
        is_old, old, new = block_copy(slot)

        @pl.when(is_old)
        def _():
            old.start()

        @pl.when(jnp.logical_not(is_old))
        def _():
            new.start()
        return carry

    def wait(slot, carry):
        is_old, old, new = block_copy(slot)

        @pl.when(is_old)
        def _():
            old.wait()

        @pl.when(jnp.logical_not(is_old))
        def _():
            new.wait()
        return carry

    lax.fori_loop(0, n_slots, start, 0)

    q = q_ref[...]
    rows = q.shape[0]
    q_pos, slope = _sample_rows(t_new, past_len, slope_ref, g)

    wc = wc_ref.shape[0] // rpt
    n_win = kwin.shape[0]
    kwin[0:wc, :] = wc_ref[pl.ds(g, wc, stride=rpt), :]
    vwin[0:wc, :] = wc_ref[pl.ds(NSA_KV_HEADS + g, wc, stride=rpt), :]
    kwin[wc:n_win, :] = jnp.zeros((n_win - wc, HEAD_DIM), F32)
    vwin[wc:n_win, :] = jnp.zeros((n_win - wc, HEAD_DIM), F32)
    kwin[wc:wc + t_new, :] = kwn_ref[...]
    vwin[wc:wc + t_new, :] = vwn_ref[...]
    dist = q_pos - (past_len - wc + _iota((rows, n_win), 1))
    mask = (dist >= 0) & (dist < WINDOW) & (_iota((rows, n_win), 1) < wc + t_new)
    p = _softmax_rows(_dot_nt(q, kwin[...].astype(BF)) * ATT_SCALE - slope * dist.astype(F32), mask)
    o_win = _dot(p.astype(BF), vwin[...].astype(BF))

    n_keys = SEL_TOPK * SEL_BLOCK
    lane = _iota((1, n_keys), 1)
    row_t = _mod(_iota((rows, 1), 0), t_new)
    o_sel = jnp.zeros((rows, HEAD_DIM), F32)
    for t in range(t_new):
        lax.fori_loop(t * SEL_TOPK, (t + 1) * SEL_TOPK, wait, 0)
        base = t * SEL_TOPK * blk_rows
        k = kvbuf[pl.ds(base + g, n_keys, stride=rpt), :].astype(BF)
        v = kvbuf[pl.ds(base + NSA_KV_HEADS + g, n_keys, stride=rpt), :].astype(BF)
        blk = jnp.zeros((1, n_keys), jnp.int32)
        for kk in range(SEL_TOPK):
            j = idx_ref[(bi * NSA_KV_HEADS + g) * n_slots + t * SEL_TOPK + kk]
            blk = jnp.where(_div(lane, SEL_BLOCK) == kk, j, blk)
        dist = q_pos - (blk * SEL_BLOCK + _mod(lane, SEL_BLOCK))
        p = _softmax_rows(_dot_nt(q, k) * ATT_SCALE - slope * dist.astype(F32), dist >= 0)
        o_sel = jnp.where(row_t == t, _dot(p.astype(BF), v), o_sel)

    gates = jax.nn.sigmoid(gl_ref[...])
    o_ref[...] = gates[:, 0:1] * oc_ref[...] + gates[:, 1:2] * o_sel + gates[:, 2:3] * o_win


def nsa_selwin_sample(idx, page_table, slopes, q_rows, o_cmp, gate_rows, win_rows, win_new, pool_rows, new_rows,
                      *, t_new, past_len):
    b, g, rows, d = q_rows.shape
    n_pages = page_table.shape[1]
    rpt = POOL_ROWS_PER_TOKEN
    wc = win_rows.shape[0] // (b * rpt)
    n_win = -(-(wc + t_new) // LANES) * LANES
    n_slots = t_new * SEL_TOPK
    rspec = lambda last: pl.BlockSpec((None, None, rows, last), lambda bi, gi, *_: (bi, gi, 0, 0))
    kv = lambda n, slot: pl.BlockSpec((None, n, d), lambda bi, gi, *_: (bi, 0, slot * NSA_KV_HEADS + gi))
    return pl.pallas_call(
        functools.partial(_nsa_selwin_sample_body, t_new=t_new, past_len=past_len, n_pages=n_pages),
        grid_spec=pltpu.PrefetchScalarGridSpec(
            num_scalar_prefetch=2,
            grid=(b, g),
            in_specs=[pl.BlockSpec(memory_space=pltpu.SMEM), rspec(d), rspec(d), rspec(3),
                      pl.BlockSpec((wc * rpt, d), lambda bi, gi, *_: (bi, 0)), kv(t_new, 0), kv(t_new, 1),
                      pl.BlockSpec(memory_space=pl.ANY), pl.BlockSpec(memory_space=pl.ANY)],
            out_specs=rspec(d),
            scratch_shapes=[pltpu.VMEM((n_slots * SEL_BLOCK * rpt, d), F32),
                            pltpu.VMEM((n_win, d), F32), pltpu.VMEM((n_win, d), F32),
                            pltpu.SemaphoreType.DMA((t_new,))],
        ),
        out_shape=jax.ShapeDtypeStruct((b, g, rows, d), F32),
        compiler_params=_params("arbitrary", "arbitrary"),
        name="nsa_selwin_sample",
    )(idx.reshape(-1), page_table.reshape(-1), slopes, q_rows, o_cmp, gate_rows,
      win_rows, win_new, win_new, pool_rows, new_rows)


def _norm_proj_body(x_ref, g_ref, w_ref, hg_ref, o_ref, *, n_norm):
    h = _rms(x_ref[...], g_ref[...]).astype(BF)
    for c in range(w_ref.shape[1] // HEAD_DIM):
        z = _dot(h, w_ref[:, c * HEAD_DIM:(c + 1) * HEAD_DIM])
        if c < n_norm:
            z = _rms(z, hg_ref[...])
        o_ref[:, c * HEAD_DIM:(c + 1) * HEAD_DIM] = z


def norm_proj(x2d, g, w, head_gain, *, n_norm, tm):
    m, d = x2d.shape
    n = w.shape[1]
    return pl.pallas_call(
        functools.partial(_norm_proj_body, n_norm=n_norm),
        grid=(m // tm,),
        in_specs=[pl.BlockSpec((tm, d), lambda i: (i, 0)), _resident((1, d)), _resident(w.shape),
                  _resident((1, HEAD_DIM))],
        out_specs=pl.BlockSpec((tm, n), lambda i: (i, 0)),
        out_shape=jax.ShapeDtypeStruct((m, n), F32),
        compiler_params=_params("parallel"),
        name="norm_proj",
    )(x2d, g.reshape(1, d), w, head_gain.reshape(1, HEAD_DIM))


def _post_mix_body(x_ref, yc_ref, on_ref, mem_ref, wo1_ref, wo2_ref, mg_ref, wq_ref, qg_ref, wo_ref, o_ref):
    x = x_ref[...] + _dot(yc_ref[...], wo1_ref[...]) + _dot(on_ref[...], wo2_ref[...])
    h = _rms(x, mg_ref[...]).astype(BF)
    rpt = 2 * MEM_HEADS
    n_mem = mem_ref.shape[0] // rpt
    heads = []
    for hd in range(MEM_HEADS):
        sl = slice(hd * HEAD_DIM, (hd + 1) * HEAD_DIM)
        q = _rms(_dot(h, wq_ref[:, sl]), qg_ref[...]).astype(BF)
        k = mem_ref[pl.ds(hd, n_mem, stride=rpt), :].astype(BF)
        v = mem_ref[pl.ds(MEM_HEADS + hd, n_mem, stride=rpt), :].astype(BF)
        s = _dot_nt(q, k) * ATT_SCALE
        e = jnp.exp(s - jnp.max(s, axis=-1, keepdims=True))
        p = e * (1.0 / jnp.sum(e, axis=-1, keepdims=True))
        heads.append(_dot(p.astype(BF), v).astype(BF))
    o_ref[...] = x + _dot(jnp.concatenate(heads, axis=1), wo_ref[...])


def post_mix(x2d, y_conv, o_nsa, mem_rows, w_out_conv, w_out_nsa, mem_g, w_q, q_g, w_o, *, tm, rows_per_batch):
    m, d = x2d.shape
    per = rows_per_batch // tm
    row = lambda n: pl.BlockSpec((tm, n), lambda i: (i, 0))
    return pl.pallas_call(
        _post_mix_body,
        grid=(m // tm,),
        in_specs=[row(d), row(y_conv.shape[1]), row(o_nsa.shape[1]),
                  pl.BlockSpec((None,) + mem_rows.shape[1:], lambda i: (i // per, 0, 0)),
                  _resident(w_out_conv.shape), _resident(w_out_nsa.shape), _resident((1, d)),
                  _resident(w_q.shape), _resident((1, HEAD_DIM)), _resident(w_o.shape)],
        out_specs=row(d),
        out_shape=jax.ShapeDtypeStruct((m, d), F32),
        compiler_params=_params("parallel"),
        name="post_mix",
    )(x2d, y_conv, o_nsa, mem_rows, w_out_conv, w_out_nsa, mem_g.reshape(1, d), w_q, q_g.reshape(1, HEAD_DIM), w_o)


def _weights(ffn1_w1, ffn1_w3, ffn1_w2, w_in, w_out, cmp_k_w1, cmp_k_w2, cmp_v_w1, cmp_v_w2,
             w_mem_q, w_mem_kv, w_mem_o, ffn2_w1, ffn2_w3, ffn2_w2):
    n_main = 2 * D_CONV + D_Q + 3 * D_KV
    n_gate = 3 * NSA_GROUP
    w_gate = w_in[:, n_main:].reshape(D_MODEL, 3, NSA_KV_HEADS, NSA_GROUP)
    w_gate = jnp.moveaxis(w_gate, 2, 1).reshape(D_MODEL, NSA_KV_HEADS, n_gate)
    w_gate = jnp.pad(w_gate, ((0, 0), (0, 0), (0, LANES - n_gate))).reshape(D_MODEL, NSA_KV_HEADS * LANES)
    flat = lambda w: w.reshape(CMP_BLOCK * HEAD_DIM, HEAD_DIM).astype(BF)
    return dict(
        ffn1=(ffn1_w1.astype(BF), ffn1_w3.astype(BF), ffn1_w2.astype(BF)),
        ffn2=(ffn2_w1.astype(BF), ffn2_w3.astype(BF), ffn2_w2.astype(BF)),
        w_main=w_in[:, :n_main].astype(BF), w_gate=w_gate.astype(BF),
        w_out_conv=w_out[:D_CONV].astype(BF), w_out_nsa=w_out[D_CONV:].astype(BF),
        w1k=flat(cmp_k_w1), w1v=flat(cmp_v_w1), w2k=cmp_k_w2.astype(BF), w2v=cmp_v_w2.astype(BF),
        w_mem_q=w_mem_q.astype(BF), w_mem_kv=w_mem_kv.astype(BF), w_mem_o=w_mem_o.astype(BF))


def _alibi_slopes():
    return jnp.exp2(-8.0 * jnp.arange(1, NSA_HEADS + 1, dtype=F32) / NSA_HEADS)


def _pe_rows(pe):
    return jnp.broadcast_to(pe.reshape(1, CMP_BLOCK * HEAD_DIM), (8, CMP_BLOCK * HEAD_DIM))


def _mix_front(x, W, ffn1_norm, mix_norm, head_gains, conv_left, conv_w, conv_b, conv_ln_g, conv_ln_b, *, tm, tt):
    b, t, d = x.shape
    x1 = ffn_half(x.reshape(b * t, d), ffn1_norm, *W["ffn1"], tm=tm)
    u, q, kv_cmp, kv_sel, kv_win, glog = in_proj(x1, mix_norm, W["w_main"], W["w_gate"], head_gains, tm=tm)
    u = u.reshape(b, t, D_CONV)
    y_conv = conv_module(u, conv_left, conv_w, conv_b, conv_ln_g, conv_ln_b, tt=tt)
    r3 = lambda a: a.reshape(b, t, a.shape[-1])
    return x1, u, y_conv.reshape(b * t, D_CONV), r3(q), r3(kv_cmp), r3(kv_sel), r3(kv_win), r3(glog)


def _kv_out(a):
    return a.reshape(a.shape[0], a.shape[1], 2, NSA_KV_HEADS, HEAD_DIM)


def kernel(x_prompt, x_sample, cache_kv_cmp, cache_kv_sel, cache_kv_win, cache_mem_kv, state_conv,
           page_table, mem_prompt, ffn1_norm, ffn1_w1, ffn1_w3, ffn1_w2, mix_norm, w_in, w_out,
           conv_w, conv_b, conv_ln_g, conv_ln_b, q_norm, k_norm_cmp, k_norm_sel, k_norm_win,
           cmp_k_pe, cmp_k_w1, cmp_k_w2, cmp_v_pe, cmp_v_w1, cmp_v_w2, mem_norm, mem_src_norm,
           w_mem_q, w_mem_kv, mem_q_norm, mem_k_norm, w_mem_o, ffn2_norm, ffn2_w1, ffn2_w3, ffn2_w2):
    assert ffn1_norm.shape[0] == 1, "single layer"
    L = 0
    W = _weights(ffn1_w1[L], ffn1_w3[L], ffn1_w2[L], w_in[L], w_out[L], cmp_k_w1[L], cmp_k_w2[L], cmp_v_w1[L],
                 cmp_v_w2[L], w_mem_q[L], w_mem_kv[L], w_mem_o[L], ffn2_w1[L], ffn2_w3[L], ffn2_w2[L])
    head_gains = jnp.pad(jnp.stack([q_norm[L], k_norm_sel[L], k_norm_win[L]]), ((0, 5), (0, 0)))
    slopes = _alibi_slopes()
    pe_k, pe_v = _pe_rows(cmp_k_pe[L]), _pe_rows(cmp_v_pe[L])
    conv_args = (conv_w[L], conv_b[L], conv_ln_g[L], conv_ln_b[L])

    def tail(x1, y_conv, o_nsa, mem_kv, tm_mix, tm_ffn, rows_per_batch):
        x3 = post_mix(x1, y_conv, o_nsa, mem_kv, W["w_out_conv"], W["w_out_nsa"], mem_norm[L], W["w_mem_q"],
                      mem_q_norm[L], W["w_mem_o"], tm=tm_mix, rows_per_batch=rows_per_batch)
        return ffn_half(x3, ffn2_norm[L], *W["ffn2"], tm=tm_ffn)

    bp, tp, d = x_prompt.shape
    tm_p = 512 if (bp * tp) % 512 == 0 else 256
    left0 = jnp.zeros((bp, CONV_HALO, D_CONV), F32)
    x1, u, y_conv, q, kv_cmp, kv_sel, kv_win, glog = _mix_front(
        x_prompt, W, ffn1_norm[L], mix_norm[L], head_gains, left0, *conv_args, tm=tm_p, tt=256)
    n_chunks = tp // CMP_STRIDE
    terms = cmp_terms_dense(kv_cmp.reshape(bp, n_chunks, CHUNK_LANES), W["w1k"], W["w1v"], rows=n_chunks)
    kc, vct = cmp_combine(terms, jnp.zeros((bp, 8, 2 * D_KV), F32), pe_k, pe_v, W["w1k"], W["w1v"], W["w2k"],
                          W["w2v"], k_norm_cmp[L], v_transposed=True)
    n_blocks = tp // SEL_BLOCK
    ov_p = _overlap_matrix(n_chunks, n_chunks - 1, max(n_blocks, 16), n_blocks).T
    o_nsa = nsa_prompt(q, kc, vct, kv_sel, kv_win, glog, slopes, ov_p, tq=256)
    n_mem = mem_prompt.shape[1]
    mem_kv_p = norm_proj(mem_prompt.reshape(bp * n_mem, d), mem_src_norm[L], W["w_mem_kv"], mem_k_norm[L],
                         n_norm=MEM_HEADS, tm=256).reshape(bp, n_mem * 2 * MEM_HEADS, HEAD_DIM)
    y_prompt = tail(x1, y_conv, o_nsa.reshape(bp * tp, D_Q), mem_kv_p, tm_p, tm_p, tp).reshape(bp, tp, d)
    wlen = min(WINDOW, tp)
    outs_p = (_kv_out(kv_cmp), _kv_out(kv_sel), _kv_out(kv_win[:, tp - wlen:]),
              mem_kv_p.reshape(bp, n_mem, 2, MEM_HEADS, HEAD_DIM), u[:, tp - (CONV_WIDTH - 1):])

    bs, ts, _ = x_sample.shape
    n_pages = page_table.shape[1]
    past_len = n_pages * PAGE_SIZE
    state = state_conv[L]
    left = jnp.pad(state, ((0, 0), (CONV_HALO - (CONV_WIDTH - 1), 0), (0, 0)))
    x1, u, y_conv, q, kv_cmp, kv_sel, kv_win, glog = _mix_front(
        x_sample, W, ffn1_norm[L], mix_norm[L], head_gains, left, *conv_args, tm=bs * ts, tt=ts)
    terms = cmp_terms_paged(cache_kv_cmp.reshape(-1, HEAD_DIM), page_table, W["w1k"], W["w1v"])
    new_chunk = jnp.pad(kv_cmp, ((0, 0), (0, CMP_STRIDE - ts), (0, 0))).reshape(1, bs, CHUNK_LANES)
    terms_new = cmp_terms_dense(new_chunk, W["w1k"], W["w1v"], rows=bs).reshape(bs, 1, 2 * D_KV)
    kc, vc = cmp_combine(terms, jnp.pad(terms_new, ((0, 0), (0, 7), (0, 0))), pe_k, pe_v, W["w1k"], W["w1v"],
                         W["w2k"], W["w2v"], k_norm_cmp[L], v_transposed=False)
    n_cmp = past_len // CMP_STRIDE
    n_blocks = past_len // SEL_BLOCK + 1
    n_blk_pad = -(-n_blocks // LANES) * LANES
    ov_s = _overlap_matrix(n_cmp, n_cmp, n_blk_pad, n_blocks)
    rows = NSA_GROUP * ts
    to_rows = lambda a: jnp.transpose(a.reshape(bs, ts, NSA_KV_HEADS, NSA_GROUP, -1), (0, 2, 3, 1, 4)).reshape(
        bs, NSA_KV_HEADS, rows, a.shape[-1] // NSA_HEADS)
    q_rows = to_rows(q)
    o_cmp, idx = nsa_cmp_sample(q_rows, kc, vc, slopes, ov_s, t_new=ts, past_len=past_len, n_blocks=n_blocks)
    gl = glog.reshape(bs, ts, NSA_KV_HEADS, LANES)[..., :3 * NSA_GROUP].reshape(bs, ts, NSA_KV_HEADS, 3, NSA_GROUP)
    gate_rows = jnp.transpose(gl, (0, 2, 4, 1, 3)).reshape(bs, NSA_KV_HEADS, rows, 3)
    new_rows = jnp.pad(kv_sel, ((0, 0), (0, SEL_BLOCK - ts), (0, 0))).reshape(bs, -1, HEAD_DIM)
    o_rows = nsa_selwin_sample(idx, page_table, slopes, q_rows, o_cmp, gate_rows,
                               cache_kv_win.reshape(-1, HEAD_DIM), kv_win,
                               cache_kv_sel.reshape(-1, HEAD_DIM), new_rows, t_new=ts, past_len=past_len)
    o_nsa = jnp.transpose(o_rows.reshape(bs, NSA_KV_HEADS, NSA_GROUP, ts, HEAD_DIM), (0, 3, 1, 2, 4))
    o_nsa = o_nsa.reshape(bs * ts, D_Q).astype(BF)
    mem_kv_s = cache_mem_kv.reshape(bs, -1, HEAD_DIM)
    y_sample = tail(x1, y_conv, o_nsa, mem_kv_s, ts, bs * ts, ts).reshape(bs, ts, d)
    win_all = jnp.concatenate([cache_kv_win[L], _kv_out(kv_win)], axis=1)
    conv_all = jnp.concatenate([state, u], axis=1)
    outs_s = (_kv_out(kv_cmp), _kv_out(kv_sel), win_all[:, ts:], conv_all[:, ts:])

    st = lambda a: a[None]
    return (y_prompt, y_sample) + tuple(st(a) for a in outs_p) + tuple(st(a) for a in outs_s)
```

```python
import functools

import jax
import jax.numpy as jnp
import numpy as np
from jax import lax
from jax.experimental import pallas as pl
from jax.experimental.pallas import tpu as pltpu

D_MODEL = 2048
PAGE_SIZE = 128
D_CONV = D_MODEL // 2
CONV_WIDTH = 31
NSA_HEADS = 8
HEAD_DIM = 128
NSA_KV_HEADS = 2
NSA_GROUP = NSA_HEADS // NSA_KV_HEADS
CMP_BLOCK = 32
CMP_STRIDE = 16
SEL_BLOCK = 64
SEL_TOPK = 16
WINDOW = 512
MEM_HEADS = 4
NORM_EPS = 1e-6
NEG_INF = -1e30
PICKED = -3e38
MAX_FLOOR = -1e29
LOG2E = 1.4426950408889634
FORCE_BONUS = 1e4
D_Q = NSA_HEADS * HEAD_DIM
D_KV = 2 * NSA_KV_HEADS * HEAD_DIM
CHUNK_LANES = CMP_STRIDE * D_KV
ATT_SCALE = HEAD_DIM ** -0.5
LANES = 128
SUBLANES = 8
CONV_HALO = 32

VMEM_LIMIT_BYTES = 56 * 1024 * 1024
BF = jnp.bfloat16
F32 = jnp.float32


def _params(*sem):
    return pltpu.CompilerParams(dimension_semantics=sem, vmem_limit_bytes=VMEM_LIMIT_BYTES)


def _resident(shape):
    nd = len(shape)
    return pl.BlockSpec(shape, lambda *_: (0,) * nd, pipeline_mode=pl.Buffered(1))


def _rms(x, g):
    return x * lax.rsqrt(jnp.mean(x * x, axis=-1, keepdims=True) + NORM_EPS) * g


def _dot(a, b):
    return jnp.dot(a, b, preferred_element_type=F32)


def _dot_nt(a, b):
    return lax.dot_general(a, b, (((1,), (1,)), ((), ())), preferred_element_type=F32)


def _iota(shape, dim):
    return lax.broadcasted_iota(jnp.int32, shape, dim)


def _div(x, n):
    return x >> (n.bit_length() - 1)


def _mod(x, n):
    return x & (n - 1)


def _split3(x):
    hi = x.astype(BF)
    r1 = x - hi.astype(F32)
    mid = r1.astype(BF)
    lo = (r1 - mid.astype(F32)).astype(BF)
    return hi, mid, lo


def _ffn_body(x_ref, g_ref, w1_ref, w3_ref, w2_ref, o_ref, h_ref, *, n_f):
    f = pl.program_id(1)

    @pl.when(f == 0)
    def _():
        h_ref[...] = _rms(x_ref[...], g_ref[...]).astype(BF)
        o_ref[...] = jnp.zeros_like(o_ref)

    h = h_ref[...]
    a = _dot(h, w1_ref[...])
    b = _dot(h, w3_ref[...])
    act = (a * jax.nn.sigmoid(a) * b).astype(BF)
    o_ref[...] += _dot(act, w2_ref[...])

    @pl.when(f == n_f - 1)
    def _():
        o_ref[...] = x_ref[...] + 0.5 * o_ref[...]


def ffn_half(x2d, g, w1, w3, w2, *, tm, tf=512):
    m, d = x2d.shape
    n_f = w1.shape[1] // tf
    return pl.pallas_call(
        functools.partial(_ffn_body, n_f=n_f),
        grid=(m // tm, n_f),
        in_specs=[
            pl.BlockSpec((tm, d), lambda i, f: (i, 0)),
            pl.BlockSpec((1, d), lambda i, f: (0, 0)),
            pl.BlockSpec((d, tf), lambda i, f: (0, f)),
            pl.BlockSpec((d, tf), lambda i, f: (0, f)),
            pl.BlockSpec((tf, d), lambda i, f: (f, 0)),
        ],
        out_specs=pl.BlockSpec((tm, d), lambda i, f: (i, 0)),
        out_shape=jax.ShapeDtypeStruct((m, d), F32),
        scratch_shapes=[pltpu.VMEM((tm, d), BF)],
        compiler_params=_params("parallel", "arbitrary"),
        name="ffn_half",
    )(x2d, g.reshape(1, d), w1, w3, w2)


def _in_proj_body(x_ref, g_ref, w_ref, wg_ref, hg_ref, u_ref, q_ref, kc_ref, ks_ref, kw_ref, gl_ref):
    h = _rms(x_ref[...], g_ref[...]).astype(BF)
    a = _dot(h, w_ref[:, 0:D_CONV])
    gate = _dot(h, w_ref[:, D_CONV:2 * D_CONV])
    u_ref[...] = a * jax.nn.sigmoid(gate)
    o = 2 * D_CONV
    zq = _dot(h, w_ref[:, o:o + D_Q])
    for hd in range(NSA_HEADS):
        sl = slice(hd * HEAD_DIM, (hd + 1) * HEAD_DIM)
        q_ref[:, sl] = _rms(zq[:, sl], hg_ref[0:1, :]).astype(BF)
    o += D_Q
    kc_ref[...] = _dot(h, w_ref[:, o:o + D_KV])
    for n, ref in ((1, ks_ref), (2, kw_ref)):
        o += D_KV
        z = _dot(h, w_ref[:, o:o + D_KV])
        for c in range(2 * NSA_KV_HEADS):
            sl = slice(c * HEAD_DIM, (c + 1) * HEAD_DIM)
            ref[:, sl] = _rms(z[:, sl], hg_ref[n:n + 1, :]) if c < NSA_KV_HEADS else z[:, sl]
    gl_ref[...] = _dot(h, wg_ref[...])


def in_proj(x2d, g, w_main, w_gate, head_gains, *, tm):
    m, d = x2d.shape
    row = lambda n: pl.BlockSpec((tm, n), lambda i: (i, 0))
    return pl.pallas_call(
        _in_proj_body,
        grid=(m // tm,),
        in_specs=[row(d), _resident((1, d)), _resident(w_main.shape), _resident(w_gate.shape),
                  _resident(head_gains.shape)],
        out_specs=[row(D_CONV), row(D_Q), row(D_KV), row(D_KV), row(D_KV), row(2 * LANES)],
        out_shape=[jax.ShapeDtypeStruct((m, D_CONV), F32), jax.ShapeDtypeStruct((m, D_Q), BF),
                   jax.ShapeDtypeStruct((m, D_KV), F32), jax.ShapeDtypeStruct((m, D_KV), F32),
                   jax.ShapeDtypeStruct((m, D_KV), F32), jax.ShapeDtypeStruct((m, 2 * LANES), F32)],
        compiler_params=_params("parallel"),
        name="in_proj",
    )(x2d, g.reshape(1, d), w_main, w_gate, head_gains)


CONV_ROWS = 32


def _conv_body(u_ref, halo_ref, left_ref, w_ref, b_ref, lg_ref, lb_ref, y_ref, win_ref, shift_ref, *, tt, tiled):
    i = pl.program_id(1)

    @pl.when(i == 0)
    def _():
        win_ref[0:CONV_HALO, :] = left_ref[...]

    if tiled:
        @pl.when(i > 0)
        def _():
            win_ref[0:CONV_HALO, :] = halo_ref[...]

    win_ref[CONV_HALO:CONV_HALO + tt, :] = u_ref[...]
    span = CONV_HALO + tt
    win_ref[span:span + SUBLANES, :] = jnp.zeros((SUBLANES, D_CONV), F32)
    for s in range(1, SUBLANES):
        shift_ref[s - 1] = win_ref[s:s + span, :]
    first = CONV_HALO - (CONV_WIDTH - 1)
    for r0 in range(0, tt, CONV_ROWS):
        rows = min(CONV_ROWS, tt - r0)
        acc = jnp.broadcast_to(b_ref[...], (rows, D_CONV))
        for k in range(CONV_WIDTH):
            s, a = (first + k) % SUBLANES, r0 + (first + k) // SUBLANES * SUBLANES
            src = win_ref[a:a + rows, :] if s == 0 else shift_ref[s - 1, a:a + rows, :]
            acc = acc + w_ref[k:k + 1, :] * src
        xc = acc - jnp.mean(acc, axis=-1, keepdims=True)
        yn = xc * lax.rsqrt(jnp.mean(xc * xc, axis=-1, keepdims=True) + NORM_EPS) * lg_ref[...] + lb_ref[...]
        y_ref[r0:r0 + rows, :] = (yn * jax.nn.sigmoid(yn)).astype(BF)


def conv_module(u, left, conv_w, conv_b, ln_g, ln_b, *, tt):
    b, t, c = u.shape
    hb = tt // CONV_HALO if tt >= CONV_HALO else 1
    halo_rows = min(CONV_HALO, t)
    w_pad = jnp.pad(conv_w, ((0, CONV_HALO - CONV_WIDTH), (0, 0)))
    return pl.pallas_call(
        functools.partial(_conv_body, tt=tt, tiled=t > tt),
        grid=(b, t // tt),
        in_specs=[
            pl.BlockSpec((None, tt, c), lambda bi, i: (bi, i, 0)),
            pl.BlockSpec((None, halo_rows, c), lambda bi, i: (bi, jnp.maximum(i * hb - 1, 0), 0)),
            pl.BlockSpec((None, CONV_HALO, c), lambda bi, i: (bi, 0, 0)),
            _resident(w_pad.shape), _resident((1, c)), _resident((1, c)), _resident((1, c)),
        ],
        out_specs=pl.BlockSpec((None, tt, c), lambda bi, i: (bi, i, 0)),
        out_shape=jax.ShapeDtypeStruct((b, t, c), BF),
        scratch_shapes=[pltpu.VMEM((CONV_HALO + tt + SUBLANES, c), F32),
                        pltpu.VMEM((SUBLANES - 1, CONV_HALO + tt, c), F32)],
        compiler_params=_params("parallel", "arbitrary"),
        name="conv_module",
    )(u, u, left, w_pad, conv_b.reshape(1, c), ln_g.reshape(1, c), ln_b.reshape(1, c))


def _cmp_terms_body(*refs, n_src, paged):
    n_prefetch = 1 if paged else 0
    src = refs[n_prefetch:n_prefetch + n_src]
    rest = refs[n_prefetch + n_src:]
    half = CMP_STRIDE * HEAD_DIM
    n_c = 2 * NSA_KV_HEADS
    cpp = PAGE_SIZE // CMP_STRIDE
    if paged:
        wk_ref, wv_ref, o_ref, pages = rest
        per_chunk = CMP_STRIDE * n_c
        for k, s in enumerate(src):
            by_chunk = s[...].reshape(cpp, per_chunk, HEAD_DIM)
            pages[k] = pltpu.einshape("cjd->jcd", by_chunk).reshape(cpp * per_chunk, HEAD_DIM)
    else:
        wk_ref, wv_ref, o_ref = rest
    for c in range(n_c):
        if paged:
            cols = [jnp.concatenate([pages[k, (j * n_c + c) * cpp:(j * n_c + c + 1) * cpp, :] for k in range(n_src)],
                                    axis=0) for j in range(CMP_STRIDE)]
        else:
            cols = [src[0][:, j * D_KV + c * HEAD_DIM:j * D_KV + (c + 1) * HEAD_DIM] for j in range(CMP_STRIDE)]
        x = jnp.concatenate(cols, axis=1).astype(BF)
        w_ref = wk_ref if c < NSA_KV_HEADS else wv_ref
        for r in range(2):
            o_ref[:, (2 * c + r) * HEAD_DIM:(2 * c + r + 1) * HEAD_DIM] = _dot(x, w_ref[r * half:(r + 1) * half, :])


def cmp_terms_dense(x, w1k, w1v, *, rows):
    b, n, l = x.shape
    return pl.pallas_call(
        functools.partial(_cmp_terms_body, n_src=1, paged=False),
        grid=(b, n // rows),
        in_specs=[pl.BlockSpec((None, rows, l), lambda bi, i: (bi, i, 0)),
                  _resident(w1k.shape), _resident(w1v.shape)],
        out_specs=pl.BlockSpec((None, rows, 2 * D_KV), lambda bi, i: (bi, i, 0)),
        out_shape=jax.ShapeDtypeStruct((b, n, 2 * D_KV), F32),
        compiler_params=_params("parallel", "parallel"),
        name="cmp_terms_dense",
    )(x, w1k, w1v)


CMP_PAGES_PER_STEP = 32
POOL_ROWS_PER_TOKEN = D_KV // HEAD_DIM
POOL_ROWS_PER_PAGE = PAGE_SIZE * POOL_ROWS_PER_TOKEN


def cmp_terms_paged(pool_rows, page_table, w1k, w1v):
    b, n_pages = page_table.shape
    pps = min(CMP_PAGES_PER_STEP, n_pages)
    cpp = PAGE_SIZE // CMP_STRIDE
    rows = pps * cpp

    def page_spec(k):
        return pl.BlockSpec((POOL_ROWS_PER_PAGE, HEAD_DIM), lambda bi, i, pt: (pt[bi, i * pps + k], 0))

    const = lambda a: pl.BlockSpec(a.shape, lambda bi, i, pt: (0, 0), pipeline_mode=pl.Buffered(1))

    return pl.pallas_call(
        functools.partial(_cmp_terms_body, n_src=pps, paged=True),
        grid_spec=pltpu.PrefetchScalarGridSpec(
            num_scalar_prefetch=1,
            grid=(b, n_pages // pps),
            in_specs=[page_spec(k) for k in range(pps)] + [const(w1k), const(w1v)],
            out_specs=pl.BlockSpec((None, rows, 2 * D_KV), lambda bi, i, pt: (bi, i, 0)),
            scratch_shapes=[pltpu.VMEM((pps, POOL_ROWS_PER_PAGE, HEAD_DIM), F32)],
        ),
        out_shape=jax.ShapeDtypeStruct((b, n_pages * cpp, 2 * D_KV), F32),
        compiler_params=_params("parallel", "parallel"),
        name="cmp_terms_paged",
    )(page_table, *([pool_rows] * pps), w1k, w1v)


def _cmp_combine_body(u_ref, x_ref, pek_ref, pev_ref, wk1_ref, wv1_ref, wk2_ref, wv2_ref, kg_ref, kc_ref, vc_ref,
                      *, n, v_transposed):
    last = _iota((n, 1), 0) == n - 1
    for c in range(2 * NSA_KV_HEADS):
        is_k = c < NSA_KV_HEADS
        pe_ref, w1_ref, w2_ref = (pek_ref, wk1_ref, wk2_ref) if is_k else (pev_ref, wv1_ref, wv2_ref)
        h0 = _dot(pe_ref[...].astype(BF), w1_ref[...])[0:1, :]
        u0 = u_ref[:, (2 * c) * HEAD_DIM:(2 * c + 1) * HEAD_DIM]
        u1 = u_ref[:, (2 * c + 1) * HEAD_DIM:(2 * c + 2) * HEAD_DIM]
        nxt = x_ref[0:1, (2 * c + 1) * HEAD_DIM:(2 * c + 2) * HEAD_DIM]
        u1 = jnp.where(last, nxt, pltpu.roll(u1, n - 1, axis=0))
        hid = h0 + u0 + u1
        z = _dot((hid * jax.nn.sigmoid(hid)).astype(BF), w2_ref[...])
        g = c % NSA_KV_HEADS
        if is_k:
            kc_ref[g] = _rms(z, kg_ref[...]).astype(BF)
        else:
            vc_ref[g] = (z.T if v_transposed else z).astype(BF)


def cmp_combine(u, u_next, pe_k, pe_v, w1k, w1v, w2k, w2v, k_gain, *, v_transposed):
    b, n, l = u.shape
    out = jax.ShapeDtypeStruct((b, NSA_KV_HEADS, n, HEAD_DIM), BF)
    ospec = pl.BlockSpec((None, NSA_KV_HEADS, n, HEAD_DIM), lambda bi: (bi, 0, 0, 0))
    out_v = jax.ShapeDtypeStruct((b, NSA_KV_HEADS, HEAD_DIM, n), BF) if v_transposed else out
    ospec_v = pl.BlockSpec((None, NSA_KV_HEADS, HEAD_DIM, n), lambda bi: (bi, 0, 0, 0)) if v_transposed else ospec
    return pl.pallas_call(
        functools.partial(_cmp_combine_body, n=n, v_transposed=v_transposed),
        grid=(b,),
        in_specs=[pl.BlockSpec((None, n, l), lambda bi: (bi, 0, 0)),
                  pl.BlockSpec((None, 8, l), lambda bi: (bi, 0, 0)),
                  _resident(pe_k.shape), _resident(pe_v.shape), _resident(w1k.shape), _resident(w1v.shape),
                  _resident(w2k.shape), _resident(w2v.shape), _resident((1, HEAD_DIM))],
        out_specs=[ospec, ospec_v],
        out_shape=[out, out_v],
        compiler_params=_params("parallel"),
        name="cmp_combine",
    )(u, u_next, pe_k, pe_v, w1k, w1v, w2k, w2v, k_gain.reshape(1, HEAD_DIM))


def _softmax_rows(s, mask):
    s = jnp.where(mask, s, NEG_INF)
    m = jnp.max(s, axis=-1, keepdims=True)
    e = jnp.where(mask, jnp.exp(s - m), 0.0)
    l = jnp.sum(e, axis=-1, keepdims=True)
    return e * jnp.where(l > 0.0, 1.0 / l, 0.0)


def _block_scores(pg, ov, q_pos, n_blocks, axis):
    hi, mid, lo = _split3(pg)
    imp = _dot(hi, ov) + _dot(mid, ov) + _dot(lo, ov) if axis == 1 else _dot(ov, hi) + _dot(ov, mid) + _dot(ov, lo)
    j = _iota(imp.shape, axis)
    jc = _div(q_pos, SEL_BLOCK)
    valid = j * SEL_BLOCK <= q_pos
    forced = (j == 0) | (j == jc) | (j == jc - 1)
    score = jnp.where(valid, imp + jnp.where(forced, FORCE_BONUS, 0.0), NEG_INF)
    return jnp.where(j < n_blocks, score, PICKED), valid


def _top_blocks(score, axis):
    j = _iota(score.shape, axis).astype(F32)
    sel = jnp.zeros(score.shape, jnp.bool_)
    picks = []
    for _ in range(SEL_TOPK):
        m = jnp.max(score, axis=axis, keepdims=True)
        idx = jnp.min(jnp.where(score == m, j, float(score.shape[axis])), axis=axis, keepdims=True)
        one = j == idx
        sel = sel | one
        score = jnp.where(one, PICKED, score)
        picks.append(idx)
    return sel, picks


def _exp_cols(s2, mask):
    s2 = jnp.where(mask, s2, NEG_INF)
    m = jnp.maximum(jnp.max(s2, axis=0, keepdims=True), MAX_FLOOR)
    e = jnp.exp2(s2 - m)
    l = jnp.sum(e, axis=0, keepdims=True)
    return e, jnp.where(l > 0.0, 1.0 / l, 0.0)


def _softmax_cols(s2, mask):
    e, inv = _exp_cols(s2, mask)
    return e * inv


SEL_TK = 1024
WIN_TK = 256


def _nsa_prompt_body(slope_ref, q_ref, kc_ref, vct_ref, ks_ref, vs_ref, kw_ref, vw_ref, gl_ref, ovt_ref, o_ref,
                     ksb, vst, kwb, vwt, bias_ref, oc_ref, m_ref, l_ref, acc_ref, *, tq, seq, n_cmp, n_blocks):
    g = pl.program_id(1)
    qt = pl.program_id(2)
    t0 = qt * tq
    n_win = min(WINDOW + tq, seq)
    slopes = [slope_ref[g * NSA_GROUP + r] * LOG2E for r in range(NSA_GROUP)]
    key_minus_query = _iota((SEL_TK, tq), 0) - _iota((SEL_TK, tq), 1)

    @pl.when(qt == 0)
    def _():
        ksb[...] = ks_ref[...].astype(BF)
        kwb[...] = kw_ref[...].astype(BF)
        for kt in range(seq // SEL_TK):
            vst[kt] = vs_ref[kt * SEL_TK:(kt + 1) * SEL_TK, :].T.astype(BF)
        for kt in range(seq // WIN_TK):
            vwt[kt] = vw_ref[kt * WIN_TK:(kt + 1) * WIN_TK, :].T.astype(BF)
        for r in range(NSA_GROUP):
            bias_ref[r] = slopes[r] * key_minus_query.astype(F32)

    q_rows = [q_ref[:, r * HEAD_DIM:(r + 1) * HEAD_DIM] for r in range(NSA_GROUP)]
    scale = ATT_SCALE * LOG2E
    q_pos = t0 + _iota((1, tq), 1)

    kc = kc_ref[...]
    vct = vct_ref[...]
    n_pad = kc.shape[0]
    i = _iota((n_pad, tq), 0)
    dist = q_pos - (i * CMP_STRIDE + (CMP_BLOCK - 1))
    mask = (dist >= 0) & (i < n_cmp)
    dist_f = dist.astype(F32)
    pg = jnp.zeros((n_pad, tq), F32)
    for r in range(NSA_GROUP):
        p = _softmax_cols(_dot_nt(kc, q_rows[r]) * scale - slopes[r] * dist_f, mask)
        pg = pg + p
        oc_ref[r] = _dot(vct, p.astype(BF))
    score, valid = _block_scores(pg, ovt_ref[...], q_pos, n_blocks, 0)
    sel, _ = _top_blocks(score, 0)
    sel_bf = jnp.where(sel & valid, 1.0, 0.0).astype(BF)

    nb = sel_bf.shape[0]
    m_ref[...] = jnp.full(m_ref.shape, MAX_FLOOR, F32)
    l_ref[...] = jnp.zeros(l_ref.shape, F32)
    acc_ref[...] = jnp.zeros(acc_ref.shape, F32)

    def sel_tile(kt, carry):
        k0 = pl.multiple_of(kt * SEL_TK, SEL_TK)
        k = ksb[pl.ds(k0, SEL_TK), :]
        vt = vst[kt]
        expand = jnp.where(_iota((SEL_TK, nb), 1) == kt * (SEL_TK // SEL_BLOCK) + _div(_iota((SEL_TK, nb), 0), SEL_BLOCK),
                           1.0, 0.0).astype(BF)
        visible = (_dot(expand, sel_bf) > 0.5) & (key_minus_query <= t0 - k0)
        mask_bias = jnp.where(visible, 0.0, NEG_INF)
        for r in range(NSA_GROUP):
            x = _dot_nt(k, q_rows[r]) * scale + bias_ref[r] + mask_bias
            shift = slopes[r] * (t0 - k0).astype(F32)
            m_old = m_ref[r]
            m_new = jnp.maximum(m_old, jnp.max(x, axis=0, keepdims=True) - shift)
            alpha = jnp.exp2(m_old - m_new)
            p = jnp.exp2(x - (m_new + shift))
            l_ref[r] = alpha * l_ref[r] + jnp.sum(p, axis=0, keepdims=True)
            acc_ref[r] = alpha * acc_ref[r] + _dot(vt, p.astype(BF))
            m_ref[r] = m_new
        return carry

    lax.fori_loop(0, (t0 + tq + SEL_TK - 1) // SEL_TK, sel_tile, 0)

    k_lo = pl.multiple_of(jnp.maximum(t0 + tq - n_win, 0), WIN_TK)
    kw = kwb[pl.ds(k_lo, n_win), :]
    w0 = k_lo // WIN_TK
    vwin = jnp.concatenate([vwt[w0 + n] for n in range(n_win // WIN_TK)], axis=1)
    dist = q_pos - (k_lo + _iota((n_win, tq), 0))
    mask = (dist >= 0) & (dist < WINDOW)
    dist_f = dist.astype(F32)

    gates = jax.nn.sigmoid(gl_ref[...]).T
    for r in range(NSA_GROUP):
        gate = lambda c: gates[c * NSA_GROUP + r:c * NSA_GROUP + r + 1, :]
        e, inv = _exp_cols(_dot_nt(kw, q_rows[r]) * scale - slopes[r] * dist_f, mask)
        o_win = _dot(vwin, e.astype(BF)) * inv
        l = l_ref[r]
        o_sel = acc_ref[r] * jnp.where(l > 0.0, 1.0 / l, 0.0)
        o = gate(0) * oc_ref[r] + gate(1) * o_sel + gate(2) * o_win
        o_ref[:, r * HEAD_DIM:(r + 1) * HEAD_DIM] = o.T.astype(BF)


def nsa_prompt(q, kc, vct, kv_sel, kv_win, glog, slopes, overlap_t, *, tq):
    b, t, _ = q.shape
    n_pad = kc.shape[2]
    n_cmp = t // CMP_STRIDE - 1
    n_blocks = t // SEL_BLOCK
    gq = NSA_GROUP * HEAD_DIM
    kv = lambda slot: pl.BlockSpec((None, t, HEAD_DIM), lambda bi, g, i: (bi, 0, slot * NSA_KV_HEADS + g),
                                   pipeline_mode=pl.Buffered(1))
    stat = pltpu.VMEM((NSA_GROUP, 1, tq), F32)
    head_t = pltpu.VMEM((NSA_GROUP, HEAD_DIM, tq), F32)
    return pl.pallas_call(
        functools.partial(_nsa_prompt_body, tq=tq, seq=t, n_cmp=n_cmp, n_blocks=n_blocks),
        grid=(b, NSA_KV_HEADS, t // tq),
        in_specs=[pl.BlockSpec(memory_space=pltpu.SMEM),
                  pl.BlockSpec((None, tq, gq), lambda bi, g, i: (bi, i, g)),
                  pl.BlockSpec((None, None, n_pad, HEAD_DIM), lambda bi, g, i: (bi, g, 0, 0)),
                  pl.BlockSpec((None, None, HEAD_DIM, n_pad), lambda bi, g, i: (bi, g, 0, 0)),
                  kv(0), kv(1), kv(0), kv(1),
                  pl.BlockSpec((None, tq, LANES), lambda bi, g, i: (bi, i, g)),
                  pl.BlockSpec(overlap_t.shape, lambda bi, g, i: (0, 0), pipeline_mode=pl.Buffered(1))],
        out_specs=pl.BlockSpec((None, tq, gq), lambda bi, g, i: (bi, i, g)),
        out_shape=jax.ShapeDtypeStruct((b, t, D_Q), BF),
        scratch_shapes=[pltpu.VMEM((t, HEAD_DIM), BF), pltpu.VMEM((t // SEL_TK, HEAD_DIM, SEL_TK), BF),
                        pltpu.VMEM((t, HEAD_DIM), BF), pltpu.VMEM((t // WIN_TK, HEAD_DIM, WIN_TK), BF),
                        pltpu.VMEM((NSA_GROUP, SEL_TK, tq), F32), head_t, stat, stat, head_t],
        compiler_params=_params("parallel", "parallel", "arbitrary"),
        name="nsa_prompt",
    )(slopes, q, kc, vct, kv_sel, kv_sel, kv_win, kv_win, glog, overlap_t)


def _overlap_matrix(n_cmp_pad, n_cmp, n_blk_pad, n_blocks):
    i = np.arange(n_cmp_pad)[:, None]
    j = np.arange(n_blk_pad)[None, :]
    per = SEL_BLOCK // CMP_STRIDE
    lo = j * per - (CMP_BLOCK // CMP_STRIDE - 1)
    ov = (i >= lo) & (i < lo + per + CMP_BLOCK // CMP_STRIDE - 1) & (i < n_cmp) & (j < n_blocks)
    return jnp.asarray(ov, BF)


def _sample_rows(t_new, past_len, slope_ref, g):
    rows = NSA_GROUP * t_new
    ri = _iota((rows, 1), 0)
    q_pos = past_len + _mod(ri, t_new)
    slope = jnp.zeros((rows, 1), F32)
    for r in range(NSA_GROUP):
        slope = jnp.where(_div(ri, t_new) == r, slope_ref[g * NSA_GROUP + r], slope)
    return q_pos, slope


def _nsa_cmp_sample_body(slope_ref, q_ref, kc_ref, vc_ref, ov_ref, oc_ref, idx_ref, *, t_new, past_len, n_blocks):
    g = pl.program_id(1)
    q = q_ref[...]
    kc = kc_ref[...]
    n = kc.shape[0]
    q_pos, slope = _sample_rows(t_new, past_len, slope_ref, g)
    dist = q_pos - (_iota((q.shape[0], n), 1) * CMP_STRIDE + (CMP_BLOCK - 1))
    p = _softmax_rows(_dot_nt(q, kc) * ATT_SCALE - slope * dist.astype(F32), dist >= 0)
    oc_ref[...] = _dot(p.astype(BF), vc_ref[...])
    pg = p[0:t_new]
    for r in range(1, NSA_GROUP):
        pg = pg + p[r * t_new:(r + 1) * t_new]
    score, _ = _block_scores(pg, ov_ref[...], q_pos[0:t_new], n_blocks, 1)
    _, picks = _top_blocks(score, 1)
    lane = _iota((t_new, SEL_TOPK), 1)
    idx = jnp.zeros((t_new, SEL_TOPK), jnp.int32)
    for k, pick in enumerate(picks):
        idx = jnp.where(lane == k, pick.astype(jnp.int32), idx)
    idx_ref[...] = idx


def nsa_cmp_sample(q_rows, kc, vc, slopes, overlap, *, t_new, past_len, n_blocks):
    b, g, rows, d = q_rows.shape
    n = kc.shape[2]
    spec = lambda r: pl.BlockSpec((None, None, r, d), lambda bi, gi: (bi, gi, 0, 0))
    return pl.pallas_call(
        functools.partial(_nsa_cmp_sample_body, t_new=t_new, past_len=past_len, n_blocks=n_blocks),
        grid=(b, g),
        in_specs=[pl.BlockSpec(memory_space=pltpu.SMEM), spec(rows), spec(n), spec(n),
                  pl.BlockSpec(overlap.shape, lambda bi, gi: (0, 0), pipeline_mode=pl.Buffered(1))],
        out_specs=[spec(rows), pl.BlockSpec((None, None, t_new, SEL_TOPK), lambda bi, gi: (bi, gi, 0, 0))],
        out_shape=[jax.ShapeDtypeStruct((b, g, rows, d), F32), jax.ShapeDtypeStruct((b, g, t_new, SEL_TOPK), jnp.int32)],
        compiler_params=_params("parallel", "parallel"),
        name="nsa_cmp_sample",
    )(slopes, q_rows, kc, vc, overlap)


def _nsa_selwin_sample_body(idx_ref, pt_ref, slope_ref, q_ref, oc_ref, gl_ref, wc_ref, kwn_ref, vwn_ref,
                            pool_ref, new_ref, o_ref, kvbuf, kwin, vwin, sems,
                            *, t_new, past_len, n_pages):
    bi = pl.program_id(0)
    g = pl.program_id(1)
    n_slots = t_new * SEL_TOPK
    bpp = PAGE_SIZE // SEL_BLOCK
    past_blocks = n_pages * bpp
    rpt = POOL_ROWS_PER_TOKEN
    blk_rows = SEL_BLOCK * rpt

    def block_copy(slot):
        j = idx_ref[(bi * NSA_KV_HEADS + g) * n_slots + slot]
        jo = jnp.minimum(j, past_blocks - 1)
        phys = pt_ref[bi * n_pages + jo // bpp]
        row0 = pl.multiple_of((phys * PAGE_SIZE + (jo % bpp) * SEL_BLOCK) * rpt, blk_rows)
        dst = kvbuf.at[pl.ds(pl.multiple_of(slot * blk_rows, blk_rows), blk_rows), :]
        sem = sems.at[slot // SEL_TOPK]
        old = pltpu.make_async_copy(pool_ref.at[pl.ds(row0, blk_rows), :], dst, sem)
        new = pltpu.make_async_copy(new_ref.at[bi], dst, sem)
        return j < past_blocks, old, new

    def start(slot, carry):
        is_old, old, new = block_copy(slot)

        @pl.when(is_old)
        def _():
            old.start()

        @pl.when(jnp.logical_not(is_old))
        def _():
            new.start()
        return carry

    def wait(slot, carry):
        is_old, old, new = block_copy(slot)

        @pl.when(is_old)
        def _():
            old.wait()

        @pl.when(jnp.logical_not(is_old))
        def _():
            new.wait()
        return carry

    lax.fori_loop(0, n_slots, start, 0)

    q = q_ref[...]
    rows = q.shape[0]
    q_pos, slope = _sample_rows(t_new, past_len, slope_ref, g)

    wc = wc_ref.shape[0] // rpt
    n_win = kwin.shape[0]
    kwin[0:wc, :] = wc_ref[pl.ds(g, wc, stride=rpt), :]
    vwin[0:wc, :] = wc_ref[pl.ds(NSA_KV_HEADS + g, wc, stride=rpt), :]
    kwin[wc:n_win, :] = jnp.zeros((n_win - wc, HEAD_DIM), F32)
    vwin[wc:n_win, :] = jnp.zeros((n_win - wc, HEAD_DIM), F32)
    kwin[wc:wc + t_new, :] = kwn_ref[...]
    vwin[wc:wc + t_new, :] = vwn_ref[...]
    dist = q_pos - (past_len - wc + _iota((rows, n_win), 1))
    mask = (dist >= 0) & (dist < WINDOW) & (_iota((rows, n_win), 1) < wc + t_new)
    p = _softmax_rows(_dot_nt(q, kwin[...].astype(BF)) * ATT_SCALE - slope * dist.astype(F32), mask)
    o_win = _dot(p.astype(BF), vwin[...].astype(BF))

    n_keys = SEL_TOPK * SEL_BLOCK
    lane = _iota((1, n_keys), 1)
    row_t = _mod(_iota((rows, 1), 0), t_new)
    o_sel = jnp.zeros((rows, HEAD_DIM), F32)
    for t in range(t_new):
        lax.fori_loop(t * SEL_TOPK, (t + 1) * SEL_TOPK, wait, 0)
        base = t * SEL_TOPK * blk_rows
        k = kvbuf[pl.ds(base + g, n_keys, stride=rpt), :].astype(BF)
        v = kvbuf[pl.ds(base + NSA_KV_HEADS + g, n_keys, stride=rpt), :].astype(BF)
        blk = jnp.zeros((1, n_keys), jnp.int32)
        for kk in range(SEL_TOPK):
            j = idx_ref[(bi * NSA_KV_HEADS + g) * n_slots + t * SEL_TOPK + kk]
            blk = jnp.where(_div(lane, SEL_BLOCK) == kk, j, blk)
        dist = q_pos - (blk * SEL_BLOCK + _mod(lane, SEL_BLOCK))
        p = _softmax_rows(_dot_nt(q, k) * ATT_SCALE - slope * dist.astype(F32), dist >= 0)
        o_sel = jnp.where(row_t == t, _dot(p.astype(BF), v), o_sel)

    gates = jax.nn.sigmoid(gl_ref[...])
    o_ref[...] = gates[:, 0:1] * oc_ref[...] + gates[:, 1:2] * o_sel + gates[:, 2:3] * o_win


def nsa_selwin_sample(idx, page_table, slopes, q_rows, o_cmp, gate_rows, win_rows, win_new, pool_rows, new_rows,
                      *, t_new, past_len):
    b, g, rows, d = q_rows.shape
    n_pages = page_table.shape[1]
    rpt = POOL_ROWS_PER_TOKEN
    wc = win_rows.shape[0] // (b * rpt)
    n_win = -(-(wc + t_new) // LANES) * LANES
    n_slots = t_new * SEL_TOPK
    rspec = lambda last: pl.BlockSpec((None, None, rows, last), lambda bi, gi, *_: (bi, gi, 0, 0))
    kv = lambda n, slot: pl.BlockSpec((None, n, d), lambda bi, gi, *_: (bi, 0, slot * NSA_KV_HEADS + gi))
    return pl.pallas_call(
        functools.partial(_nsa_selwin_sample_body, t_new=t_new, past_len=past_len, n_pages=n_pages),
        grid_spec=pltpu.PrefetchScalarGridSpec(
            num_scalar_prefetch=2,
            grid=(b, g),
            in_specs=[pl.BlockSpec(memory_space=pltpu.SMEM), rspec(d), rspec(d), rspec(3),
                      pl.BlockSpec((wc * rpt, d), lambda bi, gi, *_: (bi, 0)), kv(t_new, 0), kv(t_new, 1),
                      pl.BlockSpec(memory_space=pl.ANY), pl.BlockSpec(memory_space=pl.ANY)],
            out_specs=rspec(d),
            scratch_shapes=[pltpu.VMEM((n_slots * SEL_BLOCK * rpt, d), F32),
                            pltpu.VMEM((n_win, d), F32), pltpu.VMEM((n_win, d), F32),
                            pltpu.SemaphoreType.DMA((t_new,))],
        ),
        out_shape=jax.ShapeDtypeStruct((b, g, rows, d), F32),
        compiler_params=_params("arbitrary", "arbitrary"),
        name="nsa_selwin_sample",
    )(idx.reshape(-1), page_table.reshape(-1), slopes, q_rows, o_cmp, gate_rows,
      win_rows, win_new, win_new, pool_rows, new_rows)


def _norm_proj_body(x_ref, g_ref, w_ref, hg_ref, o_ref, *, n_norm):
    h = _rms(x_ref[...], g_ref[...]).astype(BF)
    for c in range(w_ref.shape[1] // HEAD_DIM):
        z = _dot(h, w_ref[:, c * HEAD_DIM:(c + 1) * HEAD_DIM])
        if c < n_norm:
            z = _rms(z, hg_ref[...])
        o_ref[:, c * HEAD_DIM:(c + 1) * HEAD_DIM] = z


def norm_proj(x2d, g, w, head_gain, *, n_norm, tm):
    m, d = x2d.shape
    n = w.shape[1]
    return pl.pallas_call(
        functools.partial(_norm_proj_body, n_norm=n_norm),
        grid=(m // tm,),
        in_specs=[pl.BlockSpec((tm, d), lambda i: (i, 0)), _resident((1, d)), _resident(w.shape),
                  _resident((1, HEAD_DIM))],
        out_specs=pl.BlockSpec((tm, n), lambda i: (i, 0)),
        out_shape=jax.ShapeDtypeStruct((m, n), F32),
        compiler_params=_params("parallel"),
        name="norm_proj",
    )(x2d, g.reshape(1, d), w, head_gain.reshape(1, HEAD_DIM))


def _post_mix_body(x_ref, yc_ref, on_ref, mem_ref, wo1_ref, wo2_ref, mg_ref, wq_ref, qg_ref, wo_ref, o_ref):
    x = x_ref[...] + _dot(yc_ref[...], wo1_ref[...]) + _dot(on_ref[...], wo2_ref[...])
    h = _rms(x, mg_ref[...]).astype(BF)
    rpt = 2 * MEM_HEADS
    n_mem = mem_ref.shape[0] // rpt
    heads = []
    zq = _dot(h, wq_ref[...])
    for hd in range(MEM_HEADS):
        sl = slice(hd * HEAD_DIM, (hd + 1) * HEAD_DIM)
        q = _rms(zq[:, sl], qg_ref[...]).astype(BF)
        k = mem_ref[pl.ds(hd, n_mem, stride=rpt), :].astype(BF)
        v = mem_ref[pl.ds(MEM_HEADS + hd, n_mem, stride=rpt), :].astype(BF)
        s = _dot_nt(q, k) * ATT_SCALE
        e = jnp.exp(s - jnp.max(s, axis=-1, keepdims=True))
        p = e * (1.0 / jnp.sum(e, axis=-1, keepdims=True))
        heads.append(_dot(p.astype(BF), v).astype(BF))
    o_ref[...] = x + _dot(jnp.concatenate(heads, axis=1), wo_ref[...])


def post_mix(x2d, y_conv, o_nsa, mem_rows, w_out_conv, w_out_nsa, mem_g, w_q, q_g, w_o, *, tm, rows_per_batch):
    m, d = x2d.shape
    per = rows_per_batch // tm
    row = lambda n: pl.BlockSpec((tm, n), lambda i: (i, 0))
    return pl.pallas_call(
        _post_mix_body,
        grid=(m // tm,),
        in_specs=[row(d), row(y_conv.shape[1]), row(o_nsa.shape[1]),
                  pl.BlockSpec((None,) + mem_rows.shape[1:], lambda i: (i // per, 0, 0)),
                  _resident(w_out_conv.shape), _resident(w_out_nsa.shape), _resident((1, d)),
                  _resident(w_q.shape), _resident((1, HEAD_DIM)), _resident(w_o.shape)],
        out_specs=row(d),
        out_shape=jax.ShapeDtypeStruct((m, d), F32),
        compiler_params=_params("parallel"),
        name="post_mix",
    )(x2d, y_conv, o_nsa, mem_rows, w_out_conv, w_out_nsa, mem_g.reshape(1, d), w_q, q_g.reshape(1, HEAD_DIM), w_o)


def _weights(ffn1_w1, ffn1_w3, ffn1_w2, w_in, w_out, cmp_k_w1, cmp_k_w2, cmp_v_w1, cmp_v_w2,
             w_mem_q, w_mem_kv, w_mem_o, ffn2_w1, ffn2_w3, ffn2_w2):
    n_main = 2 * D_CONV + D_Q + 3 * D_KV
    n_gate = 3 * NSA_GROUP
    w_gate = w_in[:, n_main:].reshape(D_MODEL, 3, NSA_KV_HEADS, NSA_GROUP)
    w_gate = jnp.moveaxis(w_gate, 2, 1).reshape(D_MODEL, NSA_KV_HEADS, n_gate)
    w_gate = jnp.pad(w_gate, ((0, 0), (0, 0), (0, LANES - n_gate))).reshape(D_MODEL, NSA_KV_HEADS * LANES)
    flat = lambda w: w.reshape(CMP_BLOCK * HEAD_DIM, HEAD_DIM).astype(BF)
    return dict(
        ffn1=(ffn1_w1.astype(BF), ffn1_w3.astype(BF), ffn1_w2.astype(BF)),
        ffn2=(ffn2_w1.astype(BF), ffn2_w3.astype(BF), ffn2_w2.astype(BF)),
        w_main=w_in[:, :n_main].astype(BF), w_gate=w_gate.astype(BF),
        w_out_conv=w_out[:D_CONV].astype(BF), w_out_nsa=w_out[D_CONV:].astype(BF),
        w1k=flat(cmp_k_w1), w1v=flat(cmp_v_w1), w2k=cmp_k_w2.astype(BF), w2v=cmp_v_w2.astype(BF),
        w_mem_q=w_mem_q.astype(BF), w_mem_kv=w_mem_kv.astype(BF), w_mem_o=w_mem_o.astype(BF))


def _alibi_slopes():
    return jnp.exp2(-8.0 * jnp.arange(1, NSA_HEADS + 1, dtype=F32) / NSA_HEADS)


def _pe_rows(pe):
    return jnp.broadcast_to(pe.reshape(1, CMP_BLOCK * HEAD_DIM), (8, CMP_BLOCK * HEAD_DIM))


def _mix_front(x, W, ffn1_norm, mix_norm, head_gains, conv_left, conv_w, conv_b, conv_ln_g, conv_ln_b, *, tm, tt):
    b, t, d = x.shape
    x1 = ffn_half(x.reshape(b * t, d), ffn1_norm, *W["ffn1"], tm=tm)
    u, q, kv_cmp, kv_sel, kv_win, glog = in_proj(x1, mix_norm, W["w_main"], W["w_gate"], head_gains, tm=tm)
    u = u.reshape(b, t, D_CONV)
    y_conv = conv_module(u, conv_left, conv_w, conv_b, conv_ln_g, conv_ln_b, tt=tt)
    r3 = lambda a: a.reshape(b, t, a.shape[-1])
    return x1, u, y_conv.reshape(b * t, D_CONV), r3(q), r3(kv_cmp), r3(kv_sel), r3(kv_win), r3(glog)


def _kv_out(a):
    return a.reshape(a.shape[0], a.shape[1], 2, NSA_KV_HEADS, HEAD_DIM)


def kernel(x_prompt, x_sample, cache_kv_cmp, cache_kv_sel, cache_kv_win, cache_mem_kv, state_conv,
           page_table, mem_prompt, ffn1_norm, ffn1_w1, ffn1_w3, ffn1_w2, mix_norm, w_in, w_out,
           conv_w, conv_b, conv_ln_g, conv_ln_b, q_norm, k_norm_cmp, k_norm_sel, k_norm_win,
           cmp_k_pe, cmp_k_w1, cmp_k_w2, cmp_v_pe, cmp_v_w1, cmp_v_w2, mem_norm, mem_src_norm,
           w_mem_q, w_mem_kv, mem_q_norm, mem_k_norm, w_mem_o, ffn2_norm, ffn2_w1, ffn2_w3, ffn2_w2):
    assert ffn1_norm.shape[0] == 1, "single layer"
    L = 0
    W = _weights(ffn1_w1[L], ffn1_w3[L], ffn1_w2[L], w_in[L], w_out[L], cmp_k_w1[L], cmp_k_w2[L], cmp_v_w1[L],
                 cmp_v_w2[L], w_mem_q[L], w_mem_kv[L], w_mem_o[L], ffn2_w1[L], ffn2_w3[L], ffn2_w2[L])
    head_gains = jnp.pad(jnp.stack([q_norm[L], k_norm_sel[L], k_norm_win[L]]), ((0, 5), (0, 0)))
    slopes = _alibi_slopes()
    pe_k, pe_v = _pe_rows(cmp_k_pe[L]), _pe_rows(cmp_v_pe[L])
    conv_args = (conv_w[L], conv_b[L], conv_ln_g[L], conv_ln_b[L])

    def tail(x1, y_conv, o_nsa, mem_kv, tm_mix, tm_ffn, rows_per_batch):
        x3 = post_mix(x1, y_conv, o_nsa, mem_kv, W["w_out_conv"], W["w_out_nsa"], mem_norm[L], W["w_mem_q"],
                      mem_q_norm[L], W["w_mem_o"], tm=tm_mix, rows_per_batch=rows_per_batch)
        return ffn_half(x3, ffn2_norm[L], *W["ffn2"], tm=tm_ffn)

    bp, tp, d = x_prompt.shape
    tm_p = 512 if (bp * tp) % 512 == 0 else 256
    left0 = jnp.zeros((bp, CONV_HALO, D_CONV), F32)
    x1, u, y_conv, q, kv_cmp, kv_sel, kv_win, glog = _mix_front(
        x_prompt, W, ffn1_norm[L], mix_norm[L], head_gains, left0, *conv_args, tm=tm_p, tt=256)
    n_chunks = tp // CMP_STRIDE
    terms = cmp_terms_dense(kv_cmp.reshape(bp, n_chunks, CHUNK_LANES), W["w1k"], W["w1v"], rows=n_chunks)
    kc, vct = cmp_combine(terms, jnp.zeros((bp, 8, 2 * D_KV), F32), pe_k, pe_v, W["w1k"], W["w1v"], W["w2k"],
                          W["w2v"], k_norm_cmp[L], v_transposed=True)
    n_blocks = tp // SEL_BLOCK
    ov_p = _overlap_matrix(n_chunks, n_chunks - 1, max(n_blocks, 16), n_blocks).T
    o_nsa = nsa_prompt(q, kc, vct, kv_sel, kv_win, glog, slopes, ov_p, tq=256)
    n_mem = mem_prompt.shape[1]
    mem_kv_p = norm_proj(mem_prompt.reshape(bp * n_mem, d), mem_src_norm[L], W["w_mem_kv"], mem_k_norm[L],
                         n_norm=MEM_HEADS, tm=256).reshape(bp, n_mem * 2 * MEM_HEADS, HEAD_DIM)
    y_prompt = tail(x1, y_conv, o_nsa.reshape(bp * tp, D_Q), mem_kv_p, tm_p, tm_p, tp).reshape(bp, tp, d)
    wlen = min(WINDOW, tp)
    outs_p = (_kv_out(kv_cmp), _kv_out(kv_sel), _kv_out(kv_win[:, tp - wlen:]),
              mem_kv_p.reshape(bp, n_mem, 2, MEM_HEADS, HEAD_DIM), u[:, tp - (CONV_WIDTH - 1):])

    bs, ts, _ = x_sample.shape
    n_pages = page_table.shape[1]
    past_len = n_pages * PAGE_SIZE
    state = state_conv[L]
    left = jnp.pad(state, ((0, 0), (CONV_HALO - (CONV_WIDTH - 1), 0), (0, 0)))
    x1, u, y_conv, q, kv_cmp, kv_sel, kv_win, glog = _mix_front(
        x_sample, W, ffn1_norm[L], mix_norm[L], head_gains, left, *conv_args, tm=bs * ts, tt=ts)
    terms = cmp_terms_paged(cache_kv_cmp.reshape(-1, HEAD_DIM), page_table, W["w1k"], W["w1v"])
    new_chunk = jnp.pad(kv_cmp, ((0, 0), (0, CMP_STRIDE - ts), (0, 0))).reshape(1, bs, CHUNK_LANES)
    terms_new = cmp_terms_dense(new_chunk, W["w1k"], W["w1v"], rows=bs).reshape(bs, 1, 2 * D_KV)
    kc, vc = cmp_combine(terms, jnp.pad(terms_new, ((0, 0), (0, 7), (0, 0))), pe_k, pe_v, W["w1k"], W["w1v"],
                         W["w2k"], W["w2v"], k_norm_cmp[L], v_transposed=False)
    n_cmp = past_len // CMP_STRIDE
    n_blocks = past_len // SEL_BLOCK + 1
    n_blk_pad = -(-n_blocks // LANES) * LANES
    ov_s = _overlap_matrix(n_cmp, n_cmp, n_blk_pad, n_blocks)
    rows = NSA_GROUP * ts
    to_rows = lambda a: jnp.transpose(a.reshape(bs, ts, NSA_KV_HEADS, NSA_GROUP, -1), (0, 2, 3, 1, 4)).reshape(
        bs, NSA_KV_HEADS, rows, a.shape[-1] // NSA_HEADS)
    q_rows = to_rows(q)
    o_cmp, idx = nsa_cmp_sample(q_rows, kc, vc, slopes, ov_s, t_new=ts, past_len=past_len, n_blocks=n_blocks)
    gl = glog.reshape(bs, ts, NSA_KV_HEADS, LANES)[..., :3 * NSA_GROUP].reshape(bs, ts, NSA_KV_HEADS, 3, NSA_GROUP)
    gate_rows = jnp.transpose(gl, (0, 2, 4, 1, 3)).reshape(bs, NSA_KV_HEADS, rows, 3)
    new_rows = jnp.pad(kv_sel, ((0, 0), (0, SEL_BLOCK - ts), (0, 0))).reshape(bs, -1, HEAD_DIM)
    o_rows = nsa_selwin_sample(idx, page_table, slopes, q_rows, o_cmp, gate_rows,
                               cache_kv_win.reshape(-1, HEAD_DIM), kv_win,
                               cache_kv_sel.reshape(-1, HEAD_DIM), new_rows, t_new=ts, past_len=past_len)
    o_nsa = jnp.transpose(o_rows.reshape(bs, NSA_KV_HEADS, NSA_GROUP, ts, HEAD_DIM), (0, 3, 1, 2, 4))
    o_nsa = o_nsa.reshape(bs * ts, D_Q).astype(BF)
    mem_kv_s = cache_mem_kv.reshape(bs, -1, HEAD_DIM)
    y_sample = tail(x1, y_conv, o_nsa, mem_kv_s, ts, bs * ts, ts).reshape(bs, ts, d)
    win_all = jnp.concatenate([cache_kv_win[L], _kv_out(kv_win)], axis=1)
    conv_all = jnp.concatenate([state, u], axis=1)
    outs_s = (_kv_out(kv_cmp), _kv_out(kv_sel), win_all[:, ts:], conv_all[:, ts:])

    st = lambda a: a[None]
    return (y_prompt, y_sample) + tuple(st(a) for a in outs_p) + tuple(st(a) for a in outs_s)
```

```python
import functools

import jax
import jax.numpy as jnp
import numpy as np
from jax import lax
from jax.experimental import pallas as pl
from jax.experimental.pallas import tpu as pltpu

D_MODEL = 2048
PAGE_SIZE = 128
D_CONV = D_MODEL // 2
CONV_WIDTH = 31
NSA_HEADS = 8
HEAD_DIM = 128
NSA_KV_HEADS = 2
NSA_GROUP = NSA_HEADS // NSA_KV_HEADS
CMP_BLOCK = 32
CMP_STRIDE = 16
SEL_BLOCK = 64
SEL_TOPK = 16
WINDOW = 512
MEM_HEADS = 4
NORM_EPS = 1e-6
NEG_INF = -1e30
PICKED = -3e38
MAX_FLOOR = -1e29
LOG2E = 1.4426950408889634
FORCE_BONUS = 1e4
D_Q = NSA_HEADS * HEAD_DIM
D_KV = 2 * NSA_KV_HEADS * HEAD_DIM
CHUNK_LANES = CMP_STRIDE * D_KV
ATT_SCALE = HEAD_DIM ** -0.5
LANES = 128
SUBLANES = 8
CONV_HALO = 32

VMEM_LIMIT_BYTES = 56 * 1024 * 1024
BF = jnp.bfloat16
F32 = jnp.float32


def _params(*sem):
    return pltpu.CompilerParams(dimension_semantics=sem, vmem_limit_bytes=VMEM_LIMIT_BYTES)


def _resident(shape):
    nd = len(shape)
    return pl.BlockSpec(shape, lambda *_: (0,) * nd, pipeline_mode=pl.Buffered(1))


def _rms(x, g):
    return x * lax.rsqrt(jnp.mean(x * x, axis=-1, keepdims=True) + NORM_EPS) * g


def _dot(a, b):
    return jnp.dot(a, b, preferred_element_type=F32)


def _dot_nt(a, b):
    return lax.dot_general(a, b, (((1,), (1,)), ((), ())), preferred_element_type=F32)


def _iota(shape, dim):
    return lax.broadcasted_iota(jnp.int32, shape, dim)


def _div(x, n):
    return x >> (n.bit_length() - 1)


def _mod(x, n):
    return x & (n - 1)


def _split3(x):
    hi = x.astype(BF)
    r1 = x - hi.astype(F32)
    mid = r1.astype(BF)
    lo = (r1 - mid.astype(F32)).astype(BF)
    return hi, mid, lo


def _ffn_body(x_ref, g_ref, w1_ref, w3_ref, w2_ref, o_ref, h_ref, *, n_f):
    f = pl.program_id(1)

    @pl.when(f == 0)
    def _():
        h_ref[...] = _rms(x_ref[...], g_ref[...]).astype(BF)
        o_ref[...] = jnp.zeros_like(o_ref)

    h = h_ref[...]
    a = _dot(h, w1_ref[...])
    b = _dot(h, w3_ref[...])
    act = (a * jax.nn.sigmoid(a) * b).astype(BF)
    o_ref[...] += _dot(act, w2_ref[...])

    @pl.when(f == n_f - 1)
    def _():
        o_ref[...] = x_ref[...] + 0.5 * o_ref[...]


def ffn_half(x2d, g, w1, w3, w2, *, tm, tf=512):
    m, d = x2d.shape
    n_f = w1.shape[1] // tf
    return pl.pallas_call(
        functools.partial(_ffn_body, n_f=n_f),
        grid=(m // tm, n_f),
        in_specs=[
            pl.BlockSpec((tm, d), lambda i, f: (i, 0)),
            pl.BlockSpec((1, d), lambda i, f: (0, 0)),
            pl.BlockSpec((d, tf), lambda i, f: (0, f)),
            pl.BlockSpec((d, tf), lambda i, f: (0, f)),
            pl.BlockSpec((tf, d), lambda i, f: (f, 0)),
        ],
        out_specs=pl.BlockSpec((tm, d), lambda i, f: (i, 0)),
        out_shape=jax.ShapeDtypeStruct((m, d), F32),
        scratch_shapes=[pltpu.VMEM((tm, d), BF)],
        compiler_params=_params("parallel", "arbitrary"),
        name="ffn_half",
    )(x2d, g.reshape(1, d), w1, w3, w2)


def _in_proj_body(x_ref, g_ref, w_ref, wg_ref, hg_ref, u_ref, q_ref, kc_ref, ks_ref, kw_ref, gl_ref):
    h = _rms(x_ref[...], g_ref[...]).astype(BF)
    a = _dot(h, w_ref[:, 0:D_CONV])
    gate = _dot(h, w_ref[:, D_CONV:2 * D_CONV])
    u_ref[...] = a * jax.nn.sigmoid(gate)
    o = 2 * D_CONV
    zq = _dot(h, w_ref[:, o:o + D_Q])
    for hd in range(NSA_HEADS):
        sl = slice(hd * HEAD_DIM, (hd + 1) * HEAD_DIM)
        q_ref[:, sl] = _rms(zq[:, sl], hg_ref[0:1, :]).astype(BF)
    o += D_Q
    kc_ref[...] = _dot(h, w_ref[:, o:o + D_KV])
    for n, ref in ((1, ks_ref), (2, kw_ref)):
        o += D_KV
        z = _dot(h, w_ref[:, o:o + D_KV])
        for c in range(2 * NSA_KV_HEADS):
            sl = slice(c * HEAD_DIM, (c + 1) * HEAD_DIM)
            ref[:, sl] = _rms(z[:, sl], hg_ref[n:n + 1, :]) if c < NSA_KV_HEADS else z[:, sl]
    gl_ref[...] = _dot(h, wg_ref[...])


def in_proj(x2d, g, w_main, w_gate, head_gains, *, tm):
    m, d = x2d.shape
    row = lambda n: pl.BlockSpec((tm, n), lambda i: (i, 0))
    return pl.pallas_call(
        _in_proj_body,
        grid=(m // tm,),
        in_specs=[row(d), _resident((1, d)), _resident(w_main.shape), _resident(w_gate.shape),
                  _resident(head_gains.shape)],
        out_specs=[row(D_CONV), row(D_Q), row(D_KV), row(D_KV), row(D_KV), row(2 * LANES)],
        out_shape=[jax.ShapeDtypeStruct((m, D_CONV), F32), jax.ShapeDtypeStruct((m, D_Q), BF),
                   jax.ShapeDtypeStruct((m, D_KV), F32), jax.ShapeDtypeStruct((m, D_KV), F32),
                   jax.ShapeDtypeStruct((m, D_KV), F32), jax.ShapeDtypeStruct((m, 2 * LANES), F32)],
        compiler_params=_params("parallel"),
        name="in_proj",
    )(x2d, g.reshape(1, d), w_main, w_gate, head_gains)


CONV_ROWS = 32


def _conv_body(u_ref, halo_ref, left_ref, w_ref, b_ref, lg_ref, lb_ref, y_ref, win_ref, shift_ref, *, tt, tiled):
    i = pl.program_id(1)

    @pl.when(i == 0)
    def _():
        win_ref[0:CONV_HALO, :] = left_ref[...]

    if tiled:
        @pl.when(i > 0)
        def _():
            win_ref[0:CONV_HALO, :] = halo_ref[...]

    win_ref[CONV_HALO:CONV_HALO + tt, :] = u_ref[...]
    span = CONV_HALO + tt
    win_ref[span:span + SUBLANES, :] = jnp.zeros((SUBLANES, D_CONV), F32)
    for s in range(1, SUBLANES):
        shift_ref[s - 1] = win_ref[s:s + span, :]
    first = CONV_HALO - (CONV_WIDTH - 1)
    for r0 in range(0, tt, CONV_ROWS):
        rows = min(CONV_ROWS, tt - r0)
        acc = jnp.broadcast_to(b_ref[...], (rows, D_CONV))
        for k in range(CONV_WIDTH):
            s, a = (first + k) % SUBLANES, r0 + (first + k) // SUBLANES * SUBLANES
            src = win_ref[a:a + rows, :] if s == 0 else shift_ref[s - 1, a:a + rows, :]
            acc = acc + w_ref[k:k + 1, :] * src
        xc = acc - jnp.mean(acc, axis=-1, keepdims=True)
        yn = xc * lax.rsqrt(jnp.mean(xc * xc, axis=-1, keepdims=True) + NORM_EPS) * lg_ref[...] + lb_ref[...]
        y_ref[r0:r0 + rows, :] = (yn * jax.nn.sigmoid(yn)).astype(BF)


def conv_module(u, left, conv_w, conv_b, ln_g, ln_b, *, tt):
    b, t, c = u.shape
    hb = tt // CONV_HALO if tt >= CONV_HALO else 1
    halo_rows = min(CONV_HALO, t)
    w_pad = jnp.pad(conv_w, ((0, CONV_HALO - CONV_WIDTH), (0, 0)))
    return pl.pallas_call(
        functools.partial(_conv_body, tt=tt, tiled=t > tt),
        grid=(b, t // tt),
        in_specs=[
            pl.BlockSpec((None, tt, c), lambda bi, i: (bi, i, 0)),
            pl.BlockSpec((None, halo_rows, c), lambda bi, i: (bi, jnp.maximum(i * hb - 1, 0), 0)),
            pl.BlockSpec((None, CONV_HALO, c), lambda bi, i: (bi, 0, 0)),
            _resident(w_pad.shape), _resident((1, c)), _resident((1, c)), _resident((1, c)),
        ],
        out_specs=pl.BlockSpec((None, tt, c), lambda bi, i: (bi, i, 0)),
        out_shape=jax.ShapeDtypeStruct((b, t, c), BF),
        scratch_shapes=[pltpu.VMEM((CONV_HALO + tt + SUBLANES, c), F32),
                        pltpu.VMEM((SUBLANES - 1, CONV_HALO + tt, c), F32)],
        compiler_params=_params("parallel", "arbitrary"),
        name="conv_module",
    )(u, u, left, w_pad, conv_b.reshape(1, c), ln_g.reshape(1, c), ln_b.reshape(1, c))


def _cmp_terms_body(*refs, n_src, paged):
    n_prefetch = 1 if paged else 0
    src = refs[n_prefetch:n_prefetch + n_src]
    rest = refs[n_prefetch + n_src:]
    half = CMP_STRIDE * HEAD_DIM
    n_c = 2 * NSA_KV_HEADS
    cpp = PAGE_SIZE // CMP_STRIDE
    if paged:
        wk_ref, wv_ref, o_ref, pages = rest
        per_chunk = CMP_STRIDE * n_c
        for k, s in enumerate(src):
            by_chunk = s[...].reshape(cpp, per_chunk, HEAD_DIM)
            pages[k] = pltpu.einshape("cjd->jcd", by_chunk).reshape(cpp * per_chunk, HEAD_DIM)
    else:
        wk_ref, wv_ref, o_ref = rest
    for c in range(n_c):
        if paged:
            cols = [jnp.concatenate([pages[k, (j * n_c + c) * cpp:(j * n_c + c + 1) * cpp, :] for k in range(n_src)],
                                    axis=0) for j in range(CMP_STRIDE)]
        else:
            cols = [src[0][:, j * D_KV + c * HEAD_DIM:j * D_KV + (c + 1) * HEAD_DIM] for j in range(CMP_STRIDE)]
        x = jnp.concatenate(cols, axis=1).astype(BF)
        w_ref = wk_ref if c < NSA_KV_HEADS else wv_ref
        for r in range(2):
            o_ref[:, (2 * c + r) * HEAD_DIM:(2 * c + r + 1) * HEAD_DIM] = _dot(x, w_ref[r * half:(r + 1) * half, :])


def cmp_terms_dense(x, w1k, w1v, *, rows):
    b, n, l = x.shape
    return pl.pallas_call(
        functools.partial(_cmp_terms_body, n_src=1, paged=False),
        grid=(b, n // rows),
        in_specs=[pl.BlockSpec((None, rows, l), lambda bi, i: (bi, i, 0)),
                  _resident(w1k.shape), _resident(w1v.shape)],
        out_specs=pl.BlockSpec((None, rows, 2 * D_KV), lambda bi, i: (bi, i, 0)),
        out_shape=jax.ShapeDtypeStruct((b, n, 2 * D_KV), F32),
        compiler_params=_params("parallel", "parallel"),
        name="cmp_terms_dense",
    )(x, w1k, w1v)


CMP_PAGES_PER_STEP = 32
POOL_ROWS_PER_TOKEN = D_KV // HEAD_DIM
POOL_ROWS_PER_PAGE = PAGE_SIZE * POOL_ROWS_PER_TOKEN


def cmp_terms_paged(pool_rows, page_table, w1k, w1v):
    b, n_pages = page_table.shape
    pps = min(CMP_PAGES_PER_STEP, n_pages)
    cpp = PAGE_SIZE // CMP_STRIDE
    rows = pps * cpp

    def page_spec(k):
        return pl.BlockSpec((POOL_ROWS_PER_PAGE, HEAD_DIM), lambda bi, i, pt: (pt[bi, i * pps + k], 0))

    const = lambda a: pl.BlockSpec(a.shape, lambda bi, i, pt: (0, 0), pipeline_mode=pl.Buffered(1))

    return pl.pallas_call(
        functools.partial(_cmp_terms_body, n_src=pps, paged=True),
        grid_spec=pltpu.PrefetchScalarGridSpec(
            num_scalar_prefetch=1,
            grid=(b, n_pages // pps),
            in_specs=[page_spec(k) for k in range(pps)] + [const(w1k), const(w1v)],
            out_specs=pl.BlockSpec((None, rows, 2 * D_KV), lambda bi, i, pt: (bi, i, 0)),
            scratch_shapes=[pltpu.VMEM((pps, POOL_ROWS_PER_PAGE, HEAD_DIM), F32)],
        ),
        out_shape=jax.ShapeDtypeStruct((b, n_pages * cpp, 2 * D_KV), F32),
        compiler_params=_params("parallel", "parallel"),
        name="cmp_terms_paged",
    )(page_table, *([pool_rows] * pps), w1k, w1v)


def _cmp_combine_body(u_ref, x_ref, pek_ref, pev_ref, wk1_ref, wv1_ref, wk2_ref, wv2_ref, kg_ref, kc_ref, vc_ref,
                      *, n, v_transposed):
    last = _iota((n, 1), 0) == n - 1
    for c in range(2 * NSA_KV_HEADS):
        is_k = c < NSA_KV_HEADS
        pe_ref, w1_ref, w2_ref = (pek_ref, wk1_ref, wk2_ref) if is_k else (pev_ref, wv1_ref, wv2_ref)
        h0 = _dot(pe_ref[...].astype(BF), w1_ref[...])[0:1, :]
        u0 = u_ref[:, (2 * c) * HEAD_DIM:(2 * c + 1) * HEAD_DIM]
        u1 = u_ref[:, (2 * c + 1) * HEAD_DIM:(2 * c + 2) * HEAD_DIM]
        nxt = x_ref[0:1, (2 * c + 1) * HEAD_DIM:(2 * c + 2) * HEAD_DIM]
        u1 = jnp.where(last, nxt, pltpu.roll(u1, n - 1, axis=0))
        hid = h0 + u0 + u1
        z = _dot((hid * jax.nn.sigmoid(hid)).astype(BF), w2_ref[...])
        g = c % NSA_KV_HEADS
        if is_k:
            kc_ref[g] = _rms(z, kg_ref[...]).astype(BF)
        else:
            vc_ref[g] = (z.T if v_transposed else z).astype(BF)


def cmp_combine(u, u_next, pe_k, pe_v, w1k, w1v, w2k, w2v, k_gain, *, v_transposed):
    b, n, l = u.shape
    out = jax.ShapeDtypeStruct((b, NSA_KV_HEADS, n, HEAD_DIM), BF)
    ospec = pl.BlockSpec((None, NSA_KV_HEADS, n, HEAD_DIM), lambda bi: (bi, 0, 0, 0))
    out_v = jax.ShapeDtypeStruct((b, NSA_KV_HEADS, HEAD_DIM, n), BF) if v_transposed else out
    ospec_v = pl.BlockSpec((None, NSA_KV_HEADS, HEAD_DIM, n), lambda bi: (bi, 0, 0, 0)) if v_transposed else ospec
    return pl.pallas_call(
        functools.partial(_cmp_combine_body, n=n, v_transposed=v_transposed),
        grid=(b,),
        in_specs=[pl.BlockSpec((None, n, l), lambda bi: (bi, 0, 0)),
                  pl.BlockSpec((None, 8, l), lambda bi: (bi, 0, 0)),
                  _resident(pe_k.shape), _resident(pe_v.shape), _resident(w1k.shape), _resident(w1v.shape),
                  _resident(w2k.shape), _resident(w2v.shape), _resident((1, HEAD_DIM))],
        out_specs=[ospec, ospec_v],
        out_shape=[out, out_v],
        compiler_params=_params("parallel"),
        name="cmp_combine",
    )(u, u_next, pe_k, pe_v, w1k, w1v, w2k, w2v, k_gain.reshape(1, HEAD_DIM))


def _softmax_rows(s, mask):
    s = jnp.where(mask, s, NEG_INF)
    m = jnp.max(s, axis=-1, keepdims=True)
    e = jnp.where(mask, jnp.exp(s - m), 0.0)
    l = jnp.sum(e, axis=-1, keepdims=True)
    return e * jnp.where(l > 0.0, 1.0 / l, 0.0)


def _block_scores(pg, ov, q_pos, n_blocks, axis):
    hi, mid, lo = _split3(pg)
    imp = _dot(hi, ov) + _dot(mid, ov) + _dot(lo, ov) if axis == 1 else _dot(ov, hi) + _dot(ov, mid) + _dot(ov, lo)
    j = _iota(imp.shape, axis)
    jc = _div(q_pos, SEL_BLOCK)
    valid = j * SEL_BLOCK <= q_pos
    forced = (j == 0) | (j == jc) | (j == jc - 1)
    score = jnp.where(valid, imp + jnp.where(forced, FORCE_BONUS, 0.0), NEG_INF)
    return jnp.where(j < n_blocks, score, PICKED), valid


def _top_blocks(score, axis):
    j = _iota(score.shape, axis).astype(F32)
    sel = jnp.zeros(score.shape, jnp.bool_)
    picks = []
    for _ in range(SEL_TOPK):
        m = jnp.max(score, axis=axis, keepdims=True)
        idx = jnp.min(jnp.where(score == m, j, float(score.shape[axis])), axis=axis, keepdims=True)
        one = j == idx
        sel = sel | one
        score = jnp.where(one, PICKED, score)
        picks.append(idx)
    return sel, picks


def _exp_cols(s2, mask):
    s2 = jnp.where(mask, s2, NEG_INF)
    m = jnp.maximum(jnp.max(s2, axis=0, keepdims=True), MAX_FLOOR)
    e = jnp.exp2(s2 - m)
    l = jnp.sum(e, axis=0, keepdims=True)
    return e, jnp.where(l > 0.0, 1.0 / l, 0.0)


def _softmax_cols(s2, mask):
    e, inv = _exp_cols(s2, mask)
    return e * inv


SEL_TK = 1024
WIN_TK = 256


def _nsa_prompt_body(slope_ref, q_ref, kc_ref, vct_ref, ks_ref, vs_ref, kw_ref, vw_ref, gl_ref, ovt_ref, o_ref,
                     ksb, vst, kwb, vwt, bias_ref, oc_ref, m_ref, l_ref, acc_ref, *, tq, seq, n_cmp, n_blocks):
    g = pl.program_id(1)
    qt = pl.program_id(2)
    t0 = qt * tq
    n_win = min(WINDOW + tq, seq)
    slopes = [slope_ref[g * NSA_GROUP + r] * LOG2E for r in range(NSA_GROUP)]
    key_minus_query = _iota((SEL_TK, tq), 0) - _iota((SEL_TK, tq), 1)

    @pl.when(qt == 0)
    def _():
        ksb[...] = ks_ref[...].astype(BF)
        kwb[...] = kw_ref[...].astype(BF)
        for kt in range(seq // SEL_TK):
            vst[kt] = vs_ref[kt * SEL_TK:(kt + 1) * SEL_TK, :].T.astype(BF)
        for kt in range(seq // WIN_TK):
            vwt[kt] = vw_ref[kt * WIN_TK:(kt + 1) * WIN_TK, :].T.astype(BF)
        for r in range(NSA_GROUP):
            bias_ref[r] = slopes[r] * key_minus_query.astype(F32)

    q_rows = [q_ref[:, r * HEAD_DIM:(r + 1) * HEAD_DIM] for r in range(NSA_GROUP)]
    scale = ATT_SCALE * LOG2E
    q_pos = t0 + _iota((1, tq), 1)

    kc = kc_ref[...]
    vct = vct_ref[...]
    n_pad = kc.shape[0]
    i = _iota((n_pad, tq), 0)
    dist = q_pos - (i * CMP_STRIDE + (CMP_BLOCK - 1))
    mask = (dist >= 0) & (i < n_cmp)
    dist_f = dist.astype(F32)
    pg = jnp.zeros((n_pad, tq), F32)
    for r in range(NSA_GROUP):
        p = _softmax_cols(_dot_nt(kc, q_rows[r]) * scale - slopes[r] * dist_f, mask)
        pg = pg + p
        oc_ref[r] = _dot(vct, p.astype(BF))
    score, valid = _block_scores(pg, ovt_ref[...], q_pos, n_blocks, 0)
    sel, _ = _top_blocks(score, 0)
    sel_bf = jnp.where(sel & valid, 1.0, 0.0).astype(BF)

    nb = sel_bf.shape[0]
    m_ref[...] = jnp.full(m_ref.shape, MAX_FLOOR, F32)
    l_ref[...] = jnp.zeros(l_ref.shape, F32)
    acc_ref[...] = jnp.zeros(acc_ref.shape, F32)

    def sel_tile(kt, carry):
        k0 = pl.multiple_of(kt * SEL_TK, SEL_TK)
        k = ksb[pl.ds(k0, SEL_TK), :]
        vt = vst[kt]
        expand = jnp.where(_iota((SEL_TK, nb), 1) == kt * (SEL_TK // SEL_BLOCK) + _div(_iota((SEL_TK, nb), 0), SEL_BLOCK),
                           1.0, 0.0).astype(BF)
        visible = (_dot(expand, sel_bf) > 0.5) & (key_minus_query <= t0 - k0)
        mask_bias = jnp.where(visible, 0.0, NEG_INF)
        for r in range(NSA_GROUP):
            x = _dot_nt(k, q_rows[r]) * scale + bias_ref[r] + mask_bias
            shift = slopes[r] * (t0 - k0).astype(F32)
            m_old = m_ref[r]
            m_new = jnp.maximum(m_old, jnp.max(x, axis=0, keepdims=True) - shift)
            alpha = jnp.exp2(m_old - m_new)
            p = jnp.exp2(x - (m_new + shift))
            l_ref[r] = alpha * l_ref[r] + jnp.sum(p, axis=0, keepdims=True)
            acc_ref[r] = alpha * acc_ref[r] + _dot(vt, p.astype(BF))
            m_ref[r] = m_new
        return carry

    lax.fori_loop(0, (t0 + tq + SEL_TK - 1) // SEL_TK, sel_tile, 0)

    k_lo = pl.multiple_of(jnp.maximum(t0 + tq - n_win, 0), WIN_TK)
    kw = kwb[pl.ds(k_lo, n_win), :]
    w0 = k_lo // WIN_TK
    vwin = jnp.concatenate([vwt[w0 + n] for n in range(n_win // WIN_TK)], axis=1)
    dist = q_pos - (k_lo + _iota((n_win, tq), 0))
    mask = (dist >= 0) & (dist < WINDOW)
    dist_f = dist.astype(F32)

    gates = jax.nn.sigmoid(gl_ref[...]).T
    for r in range(NSA_GROUP):
        gate = lambda c: gates[c * NSA_GROUP + r:c * NSA_GROUP + r + 1, :]
        e, inv = _exp_cols(_dot_nt(kw, q_rows[r]) * scale - slopes[r] * dist_f, mask)
        o_win = _dot(vwin, e.astype(BF)) * inv
        l = l_ref[r]
        o_sel = acc_ref[r] * jnp.where(l > 0.0, 1.0 / l, 0.0)
        o = gate(0) * oc_ref[r] + gate(1) * o_sel + gate(2) * o_win
        o_ref[:, r * HEAD_DIM:(r + 1) * HEAD_DIM] = o.T.astype(BF)


def nsa_prompt(q, kc, vct, kv_sel, kv_win, glog, slopes, overlap_t, *, tq):
    b, t, _ = q.shape
    n_pad = kc.shape[2]
    n_cmp = t // CMP_STRIDE - 1
    n_blocks = t // SEL_BLOCK
    gq = NSA_GROUP * HEAD_DIM
    kv = lambda slot: pl.BlockSpec((None, t, HEAD_DIM), lambda bi, g, i: (bi, 0, slot * NSA_KV_HEADS + g),
                                   pipeline_mode=pl.Buffered(1))
    stat = pltpu.VMEM((NSA_GROUP, 1, tq), F32)
    head_t = pltpu.VMEM((NSA_GROUP, HEAD_DIM, tq), F32)
    return pl.pallas_call(
        functools.partial(_nsa_prompt_body, tq=tq, seq=t, n_cmp=n_cmp, n_blocks=n_blocks),
        grid=(b, NSA_KV_HEADS, t // tq),
        in_specs=[pl.BlockSpec(memory_space=pltpu.SMEM),
                  pl.BlockSpec((None, tq, gq), lambda bi, g, i: (bi, i, g)),
                  pl.BlockSpec((None, None, n_pad, HEAD_DIM), lambda bi, g, i: (bi, g, 0, 0)),
                  pl.BlockSpec((None, None, HEAD_DIM, n_pad), lambda bi, g, i: (bi, g, 0, 0)),
                  kv(0), kv(1), kv(0), kv(1),
                  pl.BlockSpec((None, tq, LANES), lambda bi, g, i: (bi, i, g)),
                  pl.BlockSpec(overlap_t.shape, lambda bi, g, i: (0, 0), pipeline_mode=pl.Buffered(1))],
        out_specs=pl.BlockSpec((None, tq, gq), lambda bi, g, i: (bi, i, g)),
        out_shape=jax.ShapeDtypeStruct((b, t, D_Q), BF),
        scratch_shapes=[pltpu.VMEM((t, HEAD_DIM), BF), pltpu.VMEM((t // SEL_TK, HEAD_DIM, SEL_TK), BF),
                        pltpu.VMEM((t, HEAD_DIM), BF), pltpu.VMEM((t // WIN_TK, HEAD_DIM, WIN_TK), BF),
                        pltpu.VMEM((NSA_GROUP, SEL_TK, tq), F32), head_t, stat, stat, head_t],
        compiler_params=_params("parallel", "parallel", "arbitrary"),
        name="nsa_prompt",
    )(slopes, q, kc, vct, kv_sel, kv_sel, kv_win, kv_win, glog, overlap_t)


def _overlap_matrix(n_cmp_pad, n_cmp, n_blk_pad, n_blocks):
    i = np.arange(n_cmp_pad)[:, None]
    j = np.arange(n_blk_pad)[None, :]
    per = SEL_BLOCK // CMP_STRIDE
    lo = j * per - (CMP_BLOCK // CMP_STRIDE - 1)
    ov = (i >= lo) & (i < lo + per + CMP_BLOCK // CMP_STRIDE - 1) & (i < n_cmp) & (j < n_blocks)
    return jnp.asarray(ov, BF)


def _sample_rows(t_new, past_len, slope_ref, g):
    rows = NSA_GROUP * t_new
    ri = _iota((rows, 1), 0)
    q_pos = past_len + _mod(ri, t_new)
    slope = jnp.zeros((rows, 1), F32)
    for r in range(NSA_GROUP):
        slope = jnp.where(_div(ri, t_new) == r, slope_ref[g * NSA_GROUP + r], slope)
    return q_pos, slope


def _nsa_cmp_sample_body(slope_ref, q_ref, kc_ref, vc_ref, ov_ref, oc_ref, idx_ref, *, t_new, past_len, n_blocks):
    rows = NSA_GROUP * t_new
    n = kc_ref.shape[1]
    lane = _iota((NSA_KV_HEADS * t_new, SEL_TOPK), 1)
    scores = []
    for g in range(NSA_KV_HEADS):
        q_pos, slope = _sample_rows(t_new, past_len, slope_ref, g)
        dist = q_pos - (_iota((rows, n), 1) * CMP_STRIDE + (CMP_BLOCK - 1))
        p = _softmax_rows(_dot_nt(q_ref[g], kc_ref[g]) * ATT_SCALE - slope * dist.astype(F32), dist >= 0)
        oc_ref[g] = _dot(p.astype(BF), vc_ref[g])
        pg = p[0:t_new]
        for r in range(1, NSA_GROUP):
            pg = pg + p[r * t_new:(r + 1) * t_new]
        scores.append(_block_scores(pg, ov_ref[...], q_pos[0:t_new], n_blocks, 1)[0])
    _, picks = _top_blocks(jnp.concatenate(scores, axis=0), 1)
    idx = jnp.zeros(lane.shape, jnp.int32)
    for k, pick in enumerate(picks):
        idx = jnp.where(lane == k, pick.astype(jnp.int32), idx)
    for g in range(NSA_KV_HEADS):
        idx_ref[g] = idx[g * t_new:(g + 1) * t_new]


def nsa_cmp_sample(q_rows, kc, vc, slopes, overlap, *, t_new, past_len, n_blocks):
    b, g, rows, d = q_rows.shape
    n = kc.shape[2]
    spec = lambda r, last: pl.BlockSpec((None, g, r, last), lambda bi: (bi, 0, 0, 0))
    return pl.pallas_call(
        functools.partial(_nsa_cmp_sample_body, t_new=t_new, past_len=past_len, n_blocks=n_blocks),
        grid=(b,),
        in_specs=[pl.BlockSpec(memory_space=pltpu.SMEM), spec(rows, d), spec(n, d), spec(n, d),
                  pl.BlockSpec(overlap.shape, lambda bi: (0, 0), pipeline_mode=pl.Buffered(1))],
        out_specs=[spec(rows, d), spec(t_new, SEL_TOPK)],
        out_shape=[jax.ShapeDtypeStruct((b, g, rows, d), F32), jax.ShapeDtypeStruct((b, g, t_new, SEL_TOPK), jnp.int32)],
        compiler_params=_params("parallel"),
        name="nsa_cmp_sample",
    )(slopes, q_rows, kc, vc, overlap)


def _nsa_selwin_sample_body(idx_ref, pt_ref, slope_ref, q_ref, oc_ref, gl_ref, wc_ref, kwn_ref, vwn_ref,
                            pool_ref, new_ref, o_ref, kvbuf, kwin, vwin, sems,
                            *, t_new, past_len, n_pages):
    bi = pl.program_id(0)
    g = pl.program_id(1)
    n_slots = t_new * SEL_TOPK
    bpp = PAGE_SIZE // SEL_BLOCK
    past_blocks = n_pages * bpp
    rpt = POOL_ROWS_PER_TOKEN
    blk_rows = SEL_BLOCK * rpt

    def block_copy(slot):
        j = idx_ref[(bi * NSA_KV_HEADS + g) * n_slots + slot]
        jo = jnp.minimum(j, past_blocks - 1)
        phys = pt_ref[bi * n_pages + jo // bpp]
        row0 = pl.multiple_of((phys * PAGE_SIZE + (jo % bpp) * SEL_BLOCK) * rpt, blk_rows)
        dst = kvbuf.at[pl.ds(pl.multiple_of(slot * blk_rows, blk_rows), blk_rows), :]
        sem = sems.at[slot // SEL_TOPK]
        old = pltpu.make_async_copy(pool_ref.at[pl.ds(row0, blk_rows), :], dst, sem)
        new = pltpu.make_async_copy(new_ref.at[bi], dst, sem)
        return j < past_blocks, old, new

    def start(slot, carry):
        is_old, old, new = block_copy(slot)

        @pl.when(is_old)
        def _():
            old.start()

        @pl.when(jnp.logical_not(is_old))
        def _():
            new.start()
        return carry

    def wait(slot, carry):
        is_old, old, new = block_copy(slot)

        @pl.when(is_old)
        def _():
            old.wait()

        @pl.when(jnp.logical_not(is_old))
        def _():
            new.wait()
        return carry

    lax.fori_loop(0, n_slots, start, 0)

    q = q_ref[...]
    rows = q.shape[0]
    q_pos, slope = _sample_rows(t_new, past_len, slope_ref, g)

    wc = wc_ref.shape[0] // rpt
    n_win = kwin.shape[0]
    kwin[0:wc, :] = wc_ref[pl.ds(g, wc, stride=rpt), :]
    vwin[0:wc, :] = wc_ref[pl.ds(NSA_KV_HEADS + g, wc, stride=rpt), :]
    kwin[wc:n_win, :] = jnp.zeros((n_win - wc, HEAD_DIM), F32)
    vwin[wc:n_win, :] = jnp.zeros((n_win - wc, HEAD_DIM), F32)
    kwin[wc:wc + t_new, :] = kwn_ref[...]
    vwin[wc:wc + t_new, :] = vwn_ref[...]
    dist = q_pos - (past_len - wc + _iota((rows, n_win), 1))
    mask = (dist >= 0) & (dist < WINDOW) & (_iota((rows, n_win), 1) < wc + t_new)
    p = _softmax_rows(_dot_nt(q, kwin[...].astype(BF)) * ATT_SCALE - slope * dist.astype(F32), mask)
    o_win = _dot(p.astype(BF), vwin[...].astype(BF))

    n_keys = SEL_TOPK * SEL_BLOCK
    lane = _iota((1, n_keys), 1)
    row_t = _mod(_iota((rows, 1), 0), t_new)
    o_sel = jnp.zeros((rows, HEAD_DIM), F32)
    for t in range(t_new):
        lax.fori_loop(t * SEL_TOPK, (t + 1) * SEL_TOPK, wait, 0)
        base = t * SEL_TOPK * blk_rows
        k = kvbuf[pl.ds(base + g, n_keys, stride=rpt), :].astype(BF)
        v = kvbuf[pl.ds(base + NSA_KV_HEADS + g, n_keys, stride=rpt), :].astype(BF)
        blk = jnp.zeros((1, n_keys), jnp.int32)
        for kk in range(SEL_TOPK):
            j = idx_ref[(bi * NSA_KV_HEADS + g) * n_slots + t * SEL_TOPK + kk]
            blk = jnp.where(_div(lane, SEL_BLOCK) == kk, j, blk)
        dist = q_pos - (blk * SEL_BLOCK + _mod(lane, SEL_BLOCK))
        p = _softmax_rows(_dot_nt(q, k) * ATT_SCALE - slope * dist.astype(F32), dist >= 0)
        o_sel = jnp.where(row_t == t, _dot(p.astype(BF), v), o_sel)

    gates = jax.nn.sigmoid(gl_ref[...])
    o_ref[...] = gates[:, 0:1] * oc_ref[...] + gates[:, 1:2] * o_sel + gates[:, 2:3] * o_win


def nsa_selwin_sample(idx, page_table, slopes, q_rows, o_cmp, gate_rows, win_rows, win_new, pool_rows, new_rows,
                      *, t_new, past_len):
    b, g, rows, d = q_rows.shape
    n_pages = page_table.shape[1]
    rpt = POOL_ROWS_PER_TOKEN
    wc = win_rows.shape[0] // (b * rpt)
    n_win = -(-(wc + t_new) // LANES) * LANES
    n_slots = t_new * SEL_TOPK
    rspec = lambda last: pl.BlockSpec((None, None, rows, last), lambda bi, gi, *_: (bi, gi, 0, 0))
    kv = lambda n, slot: pl.BlockSpec((None, n, d), lambda bi, gi, *_: (bi, 0, slot * NSA_KV_HEADS + gi))
    return pl.pallas_call(
        functools.partial(_nsa_selwin_sample_body, t_new=t_new, past_len=past_len, n_pages=n_pages),
        grid_spec=pltpu.PrefetchScalarGridSpec(
            num_scalar_prefetch=2,
            grid=(b, g),
            in_specs=[pl.BlockSpec(memory_space=pltpu.SMEM), rspec(d), rspec(d), rspec(3),
                      pl.BlockSpec((wc * rpt, d), lambda bi, gi, *_: (bi, 0)), kv(t_new, 0), kv(t_new, 1),
                      pl.BlockSpec(memory_space=pl.ANY), pl.BlockSpec(memory_space=pl.ANY)],
            out_specs=rspec(d),
            scratch_shapes=[pltpu.VMEM((n_slots * SEL_BLOCK * rpt, d), F32),
                            pltpu.VMEM((n_win, d), F32), pltpu.VMEM((n_win, d), F32),
                            pltpu.SemaphoreType.DMA((t_new,))],
        ),
        out_shape=jax.ShapeDtypeStruct((b, g, rows, d), F32),
        compiler_params=_params("arbitrary", "arbitrary"),
        name="nsa_selwin_sample",
    )(idx.reshape(-1), page_table.reshape(-1), slopes, q_rows, o_cmp, gate_rows,
      win_rows, win_new, win_new, pool_rows, new_rows)


def _norm_proj_body(x_ref, g_ref, w_ref, hg_ref, o_ref, *, n_norm):
    h = _rms(x_ref[...], g_ref[...]).astype(BF)
    for c in range(w_ref.shape[1] // HEAD_DIM):
        z = _dot(h, w_ref[:, c * HEAD_DIM:(c + 1) * HEAD_DIM])
        if c < n_norm:
            z = _rms(z, hg_ref[...])
        o_ref[:, c * HEAD_DIM:(c + 1) * HEAD_DIM] = z


def norm_proj(x2d, g, w, head_gain, *, n_norm, tm):
    m, d = x2d.shape
    n = w.shape[1]
    return pl.pallas_call(
        functools.partial(_norm_proj_body, n_norm=n_norm),
        grid=(m // tm,),
        in_specs=[pl.BlockSpec((tm, d), lambda i: (i, 0)), _resident((1, d)), _resident(w.shape),
                  _resident((1, HEAD_DIM))],
        out_specs=pl.BlockSpec((tm, n), lambda i: (i, 0)),
        out_shape=jax.ShapeDtypeStruct((m, n), F32),
        compiler_params=_params("parallel"),
        name="norm_proj",
    )(x2d, g.reshape(1, d), w, head_gain.reshape(1, HEAD_DIM))


def _post_mix_body(x_ref, yc_ref, on_ref, mem_ref, wo1_ref, wo2_ref, mg_ref, wq_ref, qg_ref, wo_ref, o_ref):
    x = x_ref[...] + _dot(yc_ref[...], wo1_ref[...]) + _dot(on_ref[...], wo2_ref[...])
    h = _rms(x, mg_ref[...]).astype(BF)
    rpt = 2 * MEM_HEADS
    n_mem = mem_ref.shape[0] // rpt
    heads = []
    zq = _dot(h, wq_ref[...])
    for hd in range(MEM_HEADS):
        sl = slice(hd * HEAD_DIM, (hd + 1) * HEAD_DIM)
        q = _rms(zq[:, sl], qg_ref[...]).astype(BF)
        k = mem_ref[pl.ds(hd, n_mem, stride=rpt), :].astype(BF)
        v = mem_ref[pl.ds(MEM_HEADS + hd, n_mem, stride=rpt), :].astype(BF)
        s = _dot_nt(q, k) * ATT_SCALE
        e = jnp.exp(s - jnp.max(s, axis=-1, keepdims=True))
        p = e * (1.0 / jnp.sum(e, axis=-1, keepdims=True))
        heads.append(_dot(p.astype(BF), v).astype(BF))
    o_ref[...] = x + _dot(jnp.concatenate(heads, axis=1), wo_ref[...])


def post_mix(x2d, y_conv, o_nsa, mem_rows, w_out_conv, w_out_nsa, mem_g, w_q, q_g, w_o, *, tm, rows_per_batch):
    m, d = x2d.shape
    per = rows_per_batch // tm
    row = lambda n: pl.BlockSpec((tm, n), lambda i: (i, 0))
    return pl.pallas_call(
        _post_mix_body,
        grid=(m // tm,),
        in_specs=[row(d), row(y_conv.shape[1]), row(o_nsa.shape[1]),
                  pl.BlockSpec((None,) + mem_rows.shape[1:], lambda i: (i // per, 0, 0)),
                  _resident(w_out_conv.shape), _resident(w_out_nsa.shape), _resident((1, d)),
                  _resident(w_q.shape), _resident((1, HEAD_DIM)), _resident(w_o.shape)],
        out_specs=row(d),
        out_shape=jax.ShapeDtypeStruct((m, d), F32),
        compiler_params=_params("parallel"),
        name="post_mix",
    )(x2d, y_conv, o_nsa, mem_rows, w_out_conv, w_out_nsa, mem_g.reshape(1, d), w_q, q_g.reshape(1, HEAD_DIM), w_o)


def _weights(ffn1_w1, ffn1_w3, ffn1_w2, w_in, w_out, cmp_k_w1, cmp_k_w2, cmp_v_w1, cmp_v_w2,
             w_mem_q, w_mem_kv, w_mem_o, ffn2_w1, ffn2_w3, ffn2_w2):
    n_main = 2 * D_CONV + D_Q + 3 * D_KV
    n_gate = 3 * NSA_GROUP
    w_gate = w_in[:, n_main:].reshape(D_MODEL, 3, NSA_KV_HEADS, NSA_GROUP)
    w_gate = jnp.moveaxis(w_gate, 2, 1).reshape(D_MODEL, NSA_KV_HEADS, n_gate)
    w_gate = jnp.pad(w_gate, ((0, 0), (0, 0), (0, LANES - n_gate))).reshape(D_MODEL, NSA_KV_HEADS * LANES)
    flat = lambda w: w.reshape(CMP_BLOCK * HEAD_DIM, HEAD_DIM).astype(BF)
    return dict(
        ffn1=(ffn1_w1.astype(BF), ffn1_w3.astype(BF), ffn1_w2.astype(BF)),
        ffn2=(ffn2_w1.astype(BF), ffn2_w3.astype(BF), ffn2_w2.astype(BF)),
        w_main=w_in[:, :n_main].astype(BF), w_gate=w_gate.astype(BF),
        w_out_conv=w_out[:D_CONV].astype(BF), w_out_nsa=w_out[D_CONV:].astype(BF),
        w1k=flat(cmp_k_w1), w1v=flat(cmp_v_w1), w2k=cmp_k_w2.astype(BF), w2v=cmp_v_w2.astype(BF),
        w_mem_q=w_mem_q.astype(BF), w_mem_kv=w_mem_kv.astype(BF), w_mem_o=w_mem_o.astype(BF))


def _alibi_slopes():
    return jnp.exp2(-8.0 * jnp.arange(1, NSA_HEADS + 1, dtype=F32) / NSA_HEADS)


def _pe_rows(pe):
    return jnp.broadcast_to(pe.reshape(1, CMP_BLOCK * HEAD_DIM), (8, CMP_BLOCK * HEAD_DIM))


def _mix_front(x, W, ffn1_norm, mix_norm, head_gains, conv_left, conv_w, conv_b, conv_ln_g, conv_ln_b, *, tm, tt):
    b, t, d = x.shape
    x1 = ffn_half(x.reshape(b * t, d), ffn1_norm, *W["ffn1"], tm=tm)
    u, q, kv_cmp, kv_sel, kv_win, glog = in_proj(x1, mix_norm, W["w_main"], W["w_gate"], head_gains, tm=tm)
    u = u.reshape(b, t, D_CONV)
    y_conv = conv_module(u, conv_left, conv_w, conv_b, conv_ln_g, conv_ln_b, tt=tt)
    r3 = lambda a: a.reshape(b, t, a.shape[-1])
    return x1, u, y_conv.reshape(b * t, D_CONV), r3(q), r3(kv_cmp), r3(kv_sel), r3(kv_win), r3(glog)


def _kv_out(a):
    return a.reshape(a.shape[0], a.shape[1], 2, NSA_KV_HEADS, HEAD_DIM)


def kernel(x_prompt, x_sample, cache_kv_cmp, cache_kv_sel, cache_kv_win, cache_mem_kv, state_conv,
           page_table, mem_prompt, ffn1_norm, ffn1_w1, ffn1_w3, ffn1_w2, mix_norm, w_in, w_out,
           conv_w, conv_b, conv_ln_g, conv_ln_b, q_norm, k_norm_cmp, k_norm_sel, k_norm_win,
           cmp_k_pe, cmp_k_w1, cmp_k_w2, cmp_v_pe, cmp_v_w1, cmp_v_w2, mem_norm, mem_src_norm,
           w_mem_q, w_mem_kv, mem_q_norm, mem_k_norm, w_mem_o, ffn2_norm, ffn2_w1, ffn2_w3, ffn2_w2):
    assert ffn1_norm.shape[0] == 1, "single layer"
    L = 0
    W = _weights(ffn1_w1[L], ffn1_w3[L], ffn1_w2[L], w_in[L], w_out[L], cmp_k_w1[L], cmp_k_w2[L], cmp_v_w1[L],
                 cmp_v_w2[L], w_mem_q[L], w_mem_kv[L], w_mem_o[L], ffn2_w1[L], ffn2_w3[L], ffn2_w2[L])
    head_gains = jnp.pad(jnp.stack([q_norm[L], k_norm_sel[L], k_norm_win[L]]), ((0, 5), (0, 0)))
    slopes = _alibi_slopes()
    pe_k, pe_v = _pe_rows(cmp_k_pe[L]), _pe_rows(cmp_v_pe[L])
    conv_args = (conv_w[L], conv_b[L], conv_ln_g[L], conv_ln_b[L])

    def tail(x1, y_conv, o_nsa, mem_kv, tm_mix, tm_ffn, rows_per_batch):
        x3 = post_mix(x1, y_conv, o_nsa, mem_kv, W["w_out_conv"], W["w_out_nsa"], mem_norm[L], W["w_mem_q"],
                      mem_q_norm[L], W["w_mem_o"], tm=tm_mix, rows_per_batch=rows_per_batch)
        return ffn_half(x3, ffn2_norm[L], *W["ffn2"], tm=tm_ffn)

    bp, tp, d = x_prompt.shape
    tm_p = 512 if (bp * tp) % 512 == 0 else 256
    left0 = jnp.zeros((bp, CONV_HALO, D_CONV), F32)
    x1, u, y_conv, q, kv_cmp, kv_sel, kv_win, glog = _mix_front(
        x_prompt, W, ffn1_norm[L], mix_norm[L], head_gains, left0, *conv_args, tm=tm_p, tt=256)
    n_chunks = tp // CMP_STRIDE
    terms = cmp_terms_dense(kv_cmp.reshape(bp, n_chunks, CHUNK_LANES), W["w1k"], W["w1v"], rows=n_chunks)
    kc, vct = cmp_combine(terms, jnp.zeros((bp, 8, 2 * D_KV), F32), pe_k, pe_v, W["w1k"], W["w1v"], W["w2k"],
                          W["w2v"], k_norm_cmp[L], v_transposed=True)
    n_blocks = tp // SEL_BLOCK
    ov_p = _overlap_matrix(n_chunks, n_chunks - 1, max(n_blocks, 16), n_blocks).T
    o_nsa = nsa_prompt(q, kc, vct, kv_sel, kv_win, glog, slopes, ov_p, tq=256)
    n_mem = mem_prompt.shape[1]
    mem_kv_p = norm_proj(mem_prompt.reshape(bp * n_mem, d), mem_src_norm[L], W["w_mem_kv"], mem_k_norm[L],
                         n_norm=MEM_HEADS, tm=256).reshape(bp, n_mem * 2 * MEM_HEADS, HEAD_DIM)
    y_prompt = tail(x1, y_conv, o_nsa.reshape(bp * tp, D_Q), mem_kv_p, tm_p, tm_p, tp).reshape(bp, tp, d)
    wlen = min(WINDOW, tp)
    outs_p = (_kv_out(kv_cmp), _kv_out(kv_sel), _kv_out(kv_win[:, tp - wlen:]),
              mem_kv_p.reshape(bp, n_mem, 2, MEM_HEADS, HEAD_DIM), u[:, tp - (CONV_WIDTH - 1):])

    bs, ts, _ = x_sample.shape
    n_pages = page_table.shape[1]
    past_len = n_pages * PAGE_SIZE
    state = state_conv[L]
    left = jnp.pad(state, ((0, 0), (CONV_HALO - (CONV_WIDTH - 1), 0), (0, 0)))
    x1, u, y_conv, q, kv_cmp, kv_sel, kv_win, glog = _mix_front(
        x_sample, W, ffn1_norm[L], mix_norm[L], head_gains, left, *conv_args, tm=bs * ts, tt=ts)
    terms = cmp_terms_paged(cache_kv_cmp.reshape(-1, HEAD_DIM), page_table, W["w1k"], W["w1v"])
    new_chunk = jnp.pad(kv_cmp, ((0, 0), (0, CMP_STRIDE - ts), (0, 0))).reshape(1, bs, CHUNK_LANES)
    terms_new = cmp_terms_dense(new_chunk, W["w1k"], W["w1v"], rows=bs).reshape(bs, 1, 2 * D_KV)
    kc, vc = cmp_combine(terms, jnp.pad(terms_new, ((0, 0), (0, 7), (0, 0))), pe_k, pe_v, W["w1k"], W["w1v"],
                         W["w2k"], W["w2v"], k_norm_cmp[L], v_transposed=False)
    n_cmp = past_len // CMP_STRIDE
    n_blocks = past_len // SEL_BLOCK + 1
    n_blk_pad = -(-n_blocks // LANES) * LANES
    ov_s = _overlap_matrix(n_cmp, n_cmp, n_blk_pad, n_blocks)
    rows = NSA_GROUP * ts
    to_rows = lambda a: jnp.transpose(a.reshape(bs, ts, NSA_KV_HEADS, NSA_GROUP, -1), (0, 2, 3, 1, 4)).reshape(
        bs, NSA_KV_HEADS, rows, a.shape[-1] // NSA_HEADS)
    q_rows = to_rows(q)
    o_cmp, idx = nsa_cmp_sample(q_rows, kc, vc, slopes, ov_s, t_new=ts, past_len=past_len, n_blocks=n_blocks)
    gl = glog.reshape(bs, ts, NSA_KV_HEADS, LANES)[..., :3 * NSA_GROUP].reshape(bs, ts, NSA_KV_HEADS, 3, NSA_GROUP)
    gate_rows = jnp.transpose(gl, (0, 2, 4, 1, 3)).reshape(bs, NSA_KV_HEADS, rows, 3)
    new_rows = jnp.pad(kv_sel, ((0, 0), (0, SEL_BLOCK - ts), (0, 0))).reshape(bs, -1, HEAD_DIM)
    o_rows = nsa_selwin_sample(idx, page_table, slopes, q_rows, o_cmp, gate_rows,
                               cache_kv_win.reshape(-1, HEAD_DIM), kv_win,
                               cache_kv_sel.reshape(-1, HEAD_DIM), new_rows, t_new=ts, past_len=past_len)
    o_nsa = jnp.transpose(o_rows.reshape(bs, NSA_KV_HEADS, NSA_GROUP, ts, HEAD_DIM), (0, 3, 1, 2, 4))
    o_nsa = o_nsa.reshape(bs * ts, D_Q).astype(BF)
    mem_kv_s = cache_mem_kv.reshape(bs, -1, HEAD_DIM)
    y_sample = tail(x1, y_conv, o_nsa, mem_kv_s, ts, bs * ts, ts).reshape(bs, ts, d)
    win_all = jnp.concatenate([cache_kv_win[L], _kv_out(kv_win)], axis=1)
    conv_all = jnp.concatenate([state, u], axis=1)
    outs_s = (_kv_out(kv_cmp), _kv_out(kv_sel), win_all[:, ts:], conv_all[:, ts:])

    st = lambda a: a[None]
    return (y_prompt, y_sample) + tuple(st(a) for a in outs_p) + tuple(st(a) for a in outs_s)
```
